```python
import jax, jax.numpy as jnp
from jax import lax
import numpy as np

D_MODEL = 1024
BATCH = 1
SEQ = 16384
DEPTH = 1
DEC_BATCH = 8
DEC_SEQ = 32
PAST_LEN = 4096

CHUNK = 64
MIX_WIDTH = D_MODEL
CONV_WIDTH = MIX_WIDTH // 2
CONV_HEADS = 8
CONV_K = 3
CONV_HIST = CONV_K - 1
POOL_WIDTH = MIX_WIDTH - CONV_WIDTH
POOL_WINDOWS = (2, 4, 8, 16)
POOL_GROUPS = len(POOL_WINDOWS)
POOL_GROUP = POOL_WIDTH // POOL_GROUPS
POOL_HIST = max(POOL_WINDOWS) - 1
IN_COLS = 3 * CONV_WIDTH + POOL_WIDTH
D_FF = 4 * D_MODEL
PLE_DIM = 256
EPS = 1e-6

kernel_name = "hybrid_conv_pool_streaming_encoder_step"


def _rms(x, g):
    x32 = x.astype(jnp.float32)
    y = x32 * lax.rsqrt(jnp.mean(x32 * x32, axis=-1, keepdims=True) + EPS) * g.astype(jnp.float32)
    return y.astype(x.dtype)


def _conv_mix(bg, cg, v, hist, w_conv):
    L = v.shape[1]
    h = cg * v
    hp = jnp.concatenate([hist.astype(h.dtype), h], axis=1)
    conv = w_conv[0] * hp[:, :L] + w_conv[1] * hp[:, 1:L + 1] + w_conv[2] * hp[:, 2:]
    return bg * conv, hp[:, -CONV_HIST:]


def _pool_mix(u, hist, start, w_pool, pool_scale):
    Bn, L, _ = u.shape
    up = jnp.concatenate([hist.astype(u.dtype), u], axis=1)
    c = jnp.cumsum(up.astype(jnp.float32), axis=1)
    c = jnp.pad(c, ((0, 0), (1, 0), (0, 0)))
    pos = start + jnp.arange(L)
    means = []
    for g, w in enumerate(POOL_WINDOWS):
        sl = slice(g * POOL_GROUP, (g + 1) * POOL_GROUP)
        s = c[:, POOL_HIST + 1:, sl] - c[:, POOL_HIST + 1 - w:POOL_HIST + 1 - w + L, sl]
        cnt = jnp.minimum(pos + 1, w).astype(jnp.float32)
        means.append(s / cnt[None, :, None])
    mean = jnp.concatenate(means, axis=-1)
    d = (mean - u.astype(jnp.float32)).astype(u.dtype).reshape(Bn, L, POOL_GROUPS, POOL_GROUP)
    y = jnp.einsum('blgc,gcd->blgd', d, w_pool).reshape(Bn, L, POOL_WIDTH) * pool_scale
    return y, up[:, -POOL_HIST:]


def _layer(x, p, conv_hist, pool_hist, start, g_mix, w_in, w_conv, w_pool, pool_scale, w_out,
           g_mlp, w_up, w_down, g_ple, w_ple, w_ple_gate):
    h = _rms(x, g_mix)
    z = h @ w_in
    bg = z[..., :CONV_WIDTH]
    cg = z[..., CONV_WIDTH:2 * CONV_WIDTH]
    v = z[..., 2 * CONV_WIDTH:3 * CONV_WIDTH]
    u = z[..., 3 * CONV_WIDTH:]
    a, new_conv = _conv_mix(bg, cg, v, conv_hist, w_conv)
    b, new_pool = _pool_mix(u, pool_hist, start, w_pool, pool_scale)
    x = x + jnp.concatenate([a, b], axis=-1) @ w_out
    f = jax.nn.relu(_rms(x, g_mlp) @ w_up)
    x = x + (f * f) @ w_down
    x = x + (p @ w_ple) * jax.nn.sigmoid(_rms(x, g_ple) @ w_ple_gate)
    return x, new_conv, new_pool


def setup_inputs(seed: int = 0) -> dict:
    key = jax.random.key(seed)
    ks = jax.random.split(key, 24)
    f32 = jnp.float32
    nrm = lambda k, s, sc: jax.random.normal(k, s, f32) * sc
    gain = lambda k, s: 1.0 + 0.05 * jax.random.normal(k, s, f32)
    return {
        "x_prompt": nrm(ks[0], (BATCH, SEQ, D_MODEL), 1.0),
        "x_sample": nrm(ks[1], (DEC_BATCH, DEC_SEQ, D_MODEL), 1.0),
        "state_conv": nrm(ks[2], (DEPTH, DEC_BATCH, CONV_HIST, CONV_WIDTH), 1.0),
        "state_pool": nrm(ks[3], (DEPTH, DEC_BATCH, POOL_HIST, POOL_WIDTH), 1.0),
        "p_prompt": nrm(ks[4], (DEPTH, BATCH, SEQ, PLE_DIM), 1.0),
        "p_sample": nrm(ks[5], (DEPTH, DEC_BATCH, DEC_SEQ, PLE_DIM), 1.0),
        "g_mix": gain(ks[6], (DEPTH, D_MODEL)),
        "w_in": nrm(ks[7], (DEPTH, D_MODEL, IN_COLS), D_MODEL ** -0.5),
        "w_conv": nrm(ks[8], (DEPTH, CONV_K, CONV_WIDTH), CONV_K ** -0.5),
        "w_pool": nrm(ks[9], (DEPTH, POOL_GROUPS, POOL_GROUP, POOL_GROUP), POOL_GROUP ** -0.5),
        "pool_scale": gain(ks[10], (DEPTH, POOL_WIDTH)),
        "w_out": nrm(ks[11], (DEPTH, MIX_WIDTH, D_MODEL), MIX_WIDTH ** -0.5),
        "g_mlp": gain(ks[12], (DEPTH, D_MODEL)),
        "w_up": nrm(ks[13], (DEPTH, D_MODEL, D_FF), D_MODEL ** -0.5),
        "w_down": nrm(ks[14], (DEPTH, D_FF, D_MODEL), D_FF ** -0.5),
        "g_ple": gain(ks[15], (DEPTH, D_MODEL)),
        "w_ple": nrm(ks[16], (DEPTH, PLE_DIM, D_MODEL), PLE_DIM ** -0.5),
        "w_ple_gate": nrm(ks[17], (DEPTH, D_MODEL, D_MODEL), D_MODEL ** -0.5),
        "g_final": gain(ks[18], (D_MODEL,)),
    }


def reference(x_prompt, x_sample, state_conv, state_pool, p_prompt, p_sample, g_mix, w_in, w_conv,
              w_pool, pool_scale, w_out, g_mlp, w_up, w_down, g_ple, w_ple, w_ple_gate, g_final):
    xp = x_prompt
    xs = x_sample
    conv_p_list, pool_p_list, conv_s_list, pool_s_list = [], [], [], []
    zero_conv = jnp.zeros((x_prompt.shape[0], CONV_HIST, CONV_WIDTH), x_prompt.dtype)
    zero_pool = jnp.zeros((x_prompt.shape[0], POOL_HIST, POOL_WIDTH), x_prompt.dtype)
    for i in range(DEPTH):
        params = (g_mix[i], w_in[i], w_conv[i], w_pool[i], pool_scale[i], w_out[i],
                  g_mlp[i], w_up[i], w_down[i], g_ple[i], w_ple[i], w_ple_gate[i])
        xp, cp, pp = _layer(xp, p_prompt[i], zero_conv, zero_pool, 0, *params)
        xs, cs, ps = _layer(xs, p_sample[i], state_conv[i], state_pool[i], PAST_LEN, *params)
        conv_p_list.append(cp)
        pool_p_list.append(pp)
        conv_s_list.append(cs)
        pool_s_list.append(ps)
    y_prompt = _rms(xp, g_final)
    y_sample = _rms(xs, g_final)
    new_conv_prompt = jnp.stack(conv_p_list)
    new_pool_prompt = jnp.stack(pool_p_list)
    new_conv_sample = jnp.stack(conv_s_list)
    new_pool_sample = jnp.stack(pool_s_list)
    return (y_prompt, y_sample, new_conv_prompt, new_pool_prompt, new_conv_sample, new_pool_sample)
```

```python
import functools

import jax
import jax.numpy as jnp
from jax import lax
from jax.experimental import pallas as pl
from jax.experimental.pallas import tpu as pltpu

D_MODEL = 1024
CONV_WIDTH = 512
CONV_HIST = 2
POOL_WIDTH = 512
POOL_WINDOWS = (2, 4, 8, 16)
POOL_GROUP = 128
POOL_HIST = 15
IN_COLS = 3 * CONV_WIDTH + POOL_WIDTH
D_FF = 4096
PLE_DIM = 256
EPS = 1e-6

U_SLOT = 16
CV_SLOT = 8
FF_CHUNK = 1024
PROMPT_TILE = 512
VMEM_LIMIT_BYTES = 56 * 1024 * 1024

F32 = jnp.float32
BF16 = jnp.bfloat16


def _rms(x, g):
    ms = jnp.mean(x * x, axis=-1, keepdims=True)
    return x * lax.rsqrt(ms + EPS) * g


def _dot(a, b):
    return jnp.dot(a, b, preferred_element_type=F32)


def _mix_segment(z, pos0, n, ubuf, cvbuf, dbuf, abbuf, wconv_ref, wpool_ref, pscale_ref):
    bg = z[:, 0:CONV_WIDTH]
    cv = z[:, CONV_WIDTH:2 * CONV_WIDTH] * z[:, 2 * CONV_WIDTH:3 * CONV_WIDTH]
    cvbuf[pl.ds(CV_SLOT, n), :] = cv
    ubuf[pl.ds(U_SLOT, n), :] = z[:, 3 * CONV_WIDTH:]

    conv = (wconv_ref[0:1, :] * cvbuf[pl.ds(CV_SLOT - 2, n), :]
            + wconv_ref[1:2, :] * cvbuf[pl.ds(CV_SLOT - 1, n), :]
            + wconv_ref[2:3, :] * cv)
    abbuf[:, 0:CONV_WIDTH] = (bg * conv).astype(BF16)

    head = min(U_SLOT, n)
    pos = pos0 + lax.broadcasted_iota(jnp.int32, (head, POOL_GROUP), 0)
    for g, w in enumerate(POOL_WINDOWS):
        cols = slice(g * POOL_GROUP, (g + 1) * POOL_GROUP)

        def window_mean_minus_u(r0, rows, inv):
            u = ubuf[pl.ds(U_SLOT + r0, rows), cols]
            s = u
            for k in range(1, w):
                s = s + ubuf[pl.ds(U_SLOT + r0 - k, rows), cols]
            return (s * inv - u).astype(BF16)

        inv_head = 1.0 / jnp.minimum(pos + 1, w).astype(F32)
        dbuf[0:head, cols] = window_mean_minus_u(0, head, inv_head)
        if n > head:
            dbuf[head:n, cols] = window_mean_minus_u(head, n - head, 1.0 / w)
        y = _dot(dbuf[:, cols], wpool_ref[g]) * pscale_ref[:, cols]
        abbuf[:, CONV_WIDTH + g * POOL_GROUP:CONV_WIDTH + (g + 1) * POOL_GROUP] = y.astype(BF16)


def _layer_tail(x, ab, p, wout_ref, gmlp_ref, wup_ref, wdown_ref, gple_ref, wple_ref,
                wgate_ref, gfinal_ref):
    x = x + _dot(ab, wout_ref[...])
    h = _rms(x, gmlp_ref[...]).astype(BF16)
    acc = x
    for c in range(D_FF // FF_CHUNK):
        f = jnp.maximum(_dot(h, wup_ref[:, c * FF_CHUNK:(c + 1) * FF_CHUNK]), 0.0)
        acc = acc + _dot((f * f).astype(BF16), wdown_ref[c * FF_CHUNK:(c + 1) * FF_CHUNK, :])
    x = acc
    h = _rms(x, gple_ref[...]).astype(BF16)
    gate = jax.nn.sigmoid(_dot(h, wgate_ref[...]))
    x = x + _dot(p.astype(BF16), wple_ref[...]) * gate
    return _rms(x, gfinal_ref[...])


def _prompt_kernel(x_ref, p_ref, gmix_ref, win_ref, wconv_ref, wpool_ref, pscale_ref, wout_ref,
                   gmlp_ref, wup_ref, wdown_ref, gple_ref, wple_ref, wgate_ref, gfinal_ref,
                   y_ref, nconv_ref, npool_ref, ubuf, cvbuf, dbuf, abbuf):
    step = pl.program_id(0)
    n = x_ref.shape[0]

    @pl.when(step == 0)
    def _():
        ubuf[0:U_SLOT, :] = jnp.zeros((U_SLOT, POOL_WIDTH), F32)
        cvbuf[0:CV_SLOT, :] = jnp.zeros((CV_SLOT, CONV_WIDTH), F32)

    x = x_ref[...]
    h = _rms(x, gmix_ref[...]).astype(BF16)
    z = _dot(h, win_ref[...])
    _mix_segment(z, step * n, n, ubuf, cvbuf, dbuf, abbuf, wconv_ref, wpool_ref, pscale_ref)
    y_ref[...] = _layer_tail(x, abbuf[...], p_ref[...], wout_ref, gmlp_ref, wup_ref, wdown_ref,
                             gple_ref, wple_ref, wgate_ref, gfinal_ref)

    nconv_ref[...] = cvbuf[pl.ds(CV_SLOT + n - CONV_HIST, CONV_HIST), :]
    npool_ref[...] = ubuf[pl.ds(U_SLOT + n - POOL_HIST, POOL_HIST), :]
    ubuf[0:U_SLOT, :] = ubuf[pl.ds(n, U_SLOT), :]
    cvbuf[0:CV_SLOT, :] = cvbuf[pl.ds(n, CV_SLOT), :]


def _sample_kernel(start, x_ref, p_ref, sconv_ref, spool_ref, gmix_ref, win_ref, wconv_ref,
                   wpool_ref, pscale_ref, wout_ref, gmlp_ref, wup_ref, wdown_ref, gple_ref,
                   wple_ref, wgate_ref, gfinal_ref, y_ref, nconv_ref, npool_ref,
                   ubuf, cvbuf, dbuf, abbuf):
    nb, hist_c, _ = sconv_ref.shape
    n = x_ref.shape[0] // nb
    x = x_ref[...]
    h = _rms(x, gmix_ref[...]).astype(BF16)
    z = _dot(h, win_ref[...])
    for b in range(nb):
        rows = slice(b * n, (b + 1) * n)
        ub, cb = ubuf.at[b], cvbuf.at[b]
        ub[pl.ds(U_SLOT - POOL_HIST, POOL_HIST), :] = spool_ref[b]
        cb[pl.ds(CV_SLOT - CONV_HIST, CONV_HIST), :] = sconv_ref[b]
        _mix_segment(z[rows, :], start, n, ub, cb, dbuf.at[rows, :], abbuf.at[rows, :],
                     wconv_ref, wpool_ref, pscale_ref)
        nconv_ref[b] = cb[pl.ds(CV_SLOT + n - CONV_HIST, CONV_HIST), :]
        npool_ref[b] = ub[pl.ds(U_SLOT + n - POOL_HIST, POOL_HIST), :]
    y_ref[...] = _layer_tail(x, abbuf[...], p_ref[...], wout_ref, gmlp_ref, wup_ref, wdown_ref,
                             gple_ref, wple_ref, wgate_ref, gfinal_ref)


def _resident(shape):
    zeros = (0,) * len(shape)
    return pl.BlockSpec(shape, lambda i: zeros, pipeline_mode=pl.Buffered(1))


def _prompt_call(x, p, params, tile):
    seq = x.shape[0]
    assert seq % tile == 0 and tile >= U_SLOT
    row_spec = lambda cols: pl.BlockSpec((tile, cols), lambda i: (i, 0))
    return pl.pallas_call(
        _prompt_kernel,
        grid=(seq // tile,),
        in_specs=[row_spec(D_MODEL), row_spec(PLE_DIM)] + [_resident(w.shape) for w in params],
        out_specs=[row_spec(D_MODEL),
                   pl.BlockSpec((CONV_HIST, CONV_WIDTH), lambda i: (0, 0)),
                   pl.BlockSpec((POOL_HIST, POOL_WIDTH), lambda i: (0, 0))],
        out_shape=[jax.ShapeDtypeStruct((seq, D_MODEL), F32),
                   jax.ShapeDtypeStruct((CONV_HIST, CONV_WIDTH), F32),
                   jax.ShapeDtypeStruct((POOL_HIST, POOL_WIDTH), F32)],
        scratch_shapes=[pltpu.VMEM((U_SLOT + tile, POOL_WIDTH), F32),
                        pltpu.VMEM((CV_SLOT + tile, CONV_WIDTH), F32),
                        pltpu.VMEM((tile, POOL_WIDTH), BF16),
                        pltpu.VMEM((tile, 2 * CONV_WIDTH), BF16)],
        compiler_params=pltpu.CompilerParams(dimension_semantics=("arbitrary",),
                                             vmem_limit_bytes=VMEM_LIMIT_BYTES),
        name="prompt_layer",
    )(x, p, *params)


def _sample_call(x, p, sconv, spool, params, start):
    rows = x.shape[0]
    nb = sconv.shape[0]
    n = rows // nb
    vmem = pl.BlockSpec(memory_space=pltpu.VMEM)
    return pl.pallas_call(
        functools.partial(_sample_kernel, start),
        in_specs=[vmem] * (4 + len(params)),
        out_specs=[vmem] * 3,
        out_shape=[jax.ShapeDtypeStruct((rows, D_MODEL), F32),
                   jax.ShapeDtypeStruct((nb, CONV_HIST, CONV_WIDTH), F32),
                   jax.ShapeDtypeStruct((nb, POOL_HIST, POOL_WIDTH), F32)],
        scratch_shapes=[pltpu.VMEM((nb, U_SLOT + n, POOL_WIDTH), F32),
                        pltpu.VMEM((nb, CV_SLOT + n, CONV_WIDTH), F32),
                        pltpu.VMEM((rows, POOL_WIDTH), BF16),
                        pltpu.VMEM((rows, 2 * CONV_WIDTH), BF16)],
        compiler_params=pltpu.CompilerParams(vmem_limit_bytes=VMEM_LIMIT_BYTES),
        name="sample_layer",
    )(x, p, sconv, spool, *params)


def kernel(x_prompt, x_sample, state_conv, state_pool, p_prompt, p_sample, g_mix, w_in, w_conv,
           w_pool, pool_scale, w_out, g_mlp, w_up, w_down, g_ple, w_ple, w_ple_gate, g_final):
    depth = g_mix.shape[0]
    assert depth == 1, "single-layer trunk"
    batch, seq, _ = x_prompt.shape
    assert batch == 1
    dec_batch, dec_seq, _ = x_sample.shape
    past_len = 4096

    row = lambda v: v.reshape(1, -1)
    params = (row(g_mix[0]), w_in[0].astype(BF16), w_conv[0], w_pool[0].astype(BF16),
              row(pool_scale[0]), w_out[0].astype(BF16), row(g_mlp[0]), w_up[0].astype(BF16),
              w_down[0].astype(BF16), row(g_ple[0]), w_ple[0].astype(BF16),
              w_ple_gate[0].astype(BF16), row(g_final))

    yp, cp, pp = _prompt_call(x_prompt.reshape(seq, D_MODEL), p_prompt.reshape(seq, PLE_DIM),
                              params, PROMPT_TILE)
    ys, cs, ps = _sample_call(x_sample.reshape(dec_batch * dec_seq, D_MODEL),
                              p_sample.reshape(dec_batch * dec_seq, PLE_DIM),
                              state_conv[0], state_pool[0], params, past_len)
    return (yp.reshape(1, seq, D_MODEL), ys.reshape(dec_batch, dec_seq, D_MODEL),
            cp.reshape(1, 1, CONV_HIST, CONV_WIDTH), pp.reshape(1, 1, POOL_HIST, POOL_WIDTH),
            cs.reshape(1, dec_batch, CONV_HIST, CONV_WIDTH),
            ps.reshape(1, dec_batch, POOL_HIST, POOL_WIDTH))
```

```python
import functools

import jax
import jax.numpy as jnp
from jax import lax
from jax.experimental import pallas as pl
from jax.experimental.pallas import tpu as pltpu

D_MODEL = 1024
CONV_WIDTH = 512
CONV_HIST = 2
POOL_WIDTH = 512
POOL_WINDOWS = (2, 4, 8, 16)
POOL_GROUP = 128
POOL_HIST = 15
IN_COLS = 3 * CONV_WIDTH + POOL_WIDTH
D_FF = 4096
PLE_DIM = 256
EPS = 1e-6

U_SLOT = 16
CV_SLOT = 8
FF_CHUNK = 1024
PROMPT_TILE = 512
VMEM_LIMIT_BYTES = 56 * 1024 * 1024

F32 = jnp.float32
BF16 = jnp.bfloat16


def _rms(x, g):
    ms = jnp.mean(x * x, axis=-1, keepdims=True)
    return x * lax.rsqrt(ms + EPS) * g


def _dot(a, b):
    return jnp.dot(a, b, preferred_element_type=F32)


def _mix_segment(zb, zc, zv, zu, pos0, n, ubuf, cvbuf, dbuf, abbuf, wconv_ref, wpool_ref,
                 pscale_ref):
    ubuf[pl.ds(U_SLOT, n), :] = zu
    s = ubuf[pl.ds(0, U_SLOT + n), :]
    head = min(U_SLOT, n)
    pos = pos0 + lax.broadcasted_iota(jnp.int32, (head, POOL_GROUP), 0)
    for g, w in enumerate(POOL_WINDOWS):
        s = s + pltpu.roll(s, w // 2, axis=0)
        cols = slice(g * POOL_GROUP, (g + 1) * POOL_GROUP)
        sg = s[U_SLOT:, 0:POOL_GROUP]
        u = zu[:, cols]
        inv_head = 1.0 / jnp.minimum(pos + 1, w).astype(F32)
        dbuf[0:head, cols] = (sg[0:head] * inv_head - u[0:head]).astype(BF16)
        if n > head:
            dbuf[head:n, cols] = (sg[head:] * (1.0 / w) - u[head:]).astype(BF16)
        s = s[:, POOL_GROUP:]

    cv = zc * zv
    cvbuf[pl.ds(CV_SLOT, n), :] = cv
    conv = (wconv_ref[0:1, :] * cvbuf[pl.ds(CV_SLOT - 2, n), :]
            + wconv_ref[1:2, :] * cvbuf[pl.ds(CV_SLOT - 1, n), :]
            + wconv_ref[2:3, :] * cv)
    abbuf[:, 0:CONV_WIDTH] = (zb * conv).astype(BF16)

    for g in range(len(POOL_WINDOWS)):
        cols = slice(g * POOL_GROUP, (g + 1) * POOL_GROUP)
        y = _dot(dbuf[:, cols], wpool_ref[g]) * pscale_ref[:, cols]
        abbuf[:, CONV_WIDTH + g * POOL_GROUP:CONV_WIDTH + (g + 1) * POOL_GROUP] = y.astype(BF16)


def _project(h, win_ref):
    sec = lambda k: _dot(h, win_ref[:, k * CONV_WIDTH:(k + 1) * CONV_WIDTH])
    zu = sec(3)
    return sec(0), sec(1), sec(2), zu


def _layer_tail(x, ab, p, wout_ref, gmlp_ref, wup_ref, wdown_ref, gple_ref, wple_ref,
                wgate_ref, gfinal_ref):
    x = x + _dot(ab, wout_ref[...])
    h = _rms(x, gmlp_ref[...]).astype(BF16)
    acc = x
    for c in range(D_FF // FF_CHUNK):
        f = jnp.maximum(_dot(h, wup_ref[:, c * FF_CHUNK:(c + 1) * FF_CHUNK]), 0.0)
        acc = acc + _dot((f * f).astype(BF16), wdown_ref[c * FF_CHUNK:(c + 1) * FF_CHUNK, :])
    x = acc
    h = _rms(x, gple_ref[...]).astype(BF16)
    gate = jax.nn.sigmoid(_dot(h, wgate_ref[...]))
    x = x + _dot(p.astype(BF16), wple_ref[...]) * gate
    return _rms(x, gfinal_ref[...])


def _prompt_kernel(x_ref, p_ref, gmix_ref, win_ref, wconv_ref, wpool_ref, pscale_ref, wout_ref,
                   gmlp_ref, wup_ref, wdown_ref, gple_ref, wple_ref, wgate_ref, gfinal_ref,
                   y_ref, nconv_ref, npool_ref, ubuf, cvbuf, dbuf, abbuf):
    step = pl.program_id(0)
    n = x_ref.shape[0]

    @pl.when(step == 0)
    def _():
        ubuf[0:U_SLOT, :] = jnp.zeros((U_SLOT, POOL_WIDTH), F32)
        cvbuf[0:CV_SLOT, :] = jnp.zeros((CV_SLOT, CONV_WIDTH), F32)

    x = x_ref[...]
    h = _rms(x, gmix_ref[...]).astype(BF16)
    zb, zc, zv, zu = _project(h, win_ref)
    _mix_segment(zb, zc, zv, zu, step * n, n, ubuf, cvbuf, dbuf, abbuf, wconv_ref, wpool_ref,
                 pscale_ref)
    y_ref[...] = _layer_tail(x, abbuf[...], p_ref[...], wout_ref, gmlp_ref, wup_ref, wdown_ref,
                             gple_ref, wple_ref, wgate_ref, gfinal_ref)

    nconv_ref[...] = cvbuf[pl.ds(CV_SLOT + n - CONV_HIST, CONV_HIST), :]
    npool_ref[...] = ubuf[pl.ds(U_SLOT + n - POOL_HIST, POOL_HIST), :]
    ubuf[0:U_SLOT, :] = ubuf[pl.ds(n, U_SLOT), :]
    cvbuf[0:CV_SLOT, :] = cvbuf[pl.ds(n, CV_SLOT), :]


def _sample_kernel(start, x_ref, p_ref, sconv_ref, spool_ref, gmix_ref, win_ref, wconv_ref,
                   wpool_ref, pscale_ref, wout_ref, gmlp_ref, wup_ref, wdown_ref, gple_ref,
                   wple_ref, wgate_ref, gfinal_ref, y_ref, nconv_ref, npool_ref,
                   ubuf, cvbuf, dbuf, abbuf):
    nb, hist_c, _ = sconv_ref.shape
    n = x_ref.shape[0] // nb
    x = x_ref[...]
    h = _rms(x, gmix_ref[...]).astype(BF16)
    zb, zc, zv, zu = _project(h, win_ref)
    for b in range(nb):
        rows = slice(b * n, (b + 1) * n)
        ub, cb = ubuf.at[b], cvbuf.at[b]
        ub[pl.ds(0, U_SLOT - POOL_HIST), :] = jnp.zeros((U_SLOT - POOL_HIST, POOL_WIDTH), F32)
        ub[pl.ds(U_SLOT - POOL_HIST, POOL_HIST), :] = spool_ref[b]
        cb[pl.ds(CV_SLOT - CONV_HIST, CONV_HIST), :] = sconv_ref[b]
        _mix_segment(zb[rows], zc[rows], zv[rows], zu[rows], start, n, ub, cb, dbuf.at[rows, :],
                     abbuf.at[rows, :], wconv_ref, wpool_ref, pscale_ref)
        nconv_ref[b] = cb[pl.ds(CV_SLOT + n - CONV_HIST, CONV_HIST), :]
        npool_ref[b] = ub[pl.ds(U_SLOT + n - POOL_HIST, POOL_HIST), :]
    y_ref[...] = _layer_tail(x, abbuf[...], p_ref[...], wout_ref, gmlp_ref, wup_ref, wdown_ref,
                             gple_ref, wple_ref, wgate_ref, gfinal_ref)


def _resident(shape):
    zeros = (0,) * len(shape)
    return pl.BlockSpec(shape, lambda i: zeros, pipeline_mode=pl.Buffered(1))


def _prompt_call(x, p, params, tile):
    seq = x.shape[0]
    assert seq % tile == 0 and tile >= U_SLOT
    row_spec = lambda cols: pl.BlockSpec((tile, cols), lambda i: (i, 0))
    return pl.pallas_call(
        _prompt_kernel,
        grid=(seq // tile,),
        in_specs=[row_spec(D_MODEL), row_spec(PLE_DIM)] + [_resident(w.shape) for w in params],
        out_specs=[row_spec(D_MODEL),
                   pl.BlockSpec((CONV_HIST, CONV_WIDTH), lambda i: (0, 0)),
                   pl.BlockSpec((POOL_HIST, POOL_WIDTH), lambda i: (0, 0))],
        out_shape=[jax.ShapeDtypeStruct((seq, D_MODEL), F32),
                   jax.ShapeDtypeStruct((CONV_HIST, CONV_WIDTH), F32),
                   jax.ShapeDtypeStruct((POOL_HIST, POOL_WIDTH), F32)],
        scratch_shapes=[pltpu.VMEM((U_SLOT + tile, POOL_WIDTH), F32),
                        pltpu.VMEM((CV_SLOT + tile, CONV_WIDTH), F32),
                        pltpu.VMEM((tile, POOL_WIDTH), BF16),
                        pltpu.VMEM((tile, 2 * CONV_WIDTH), BF16)],
        compiler_params=pltpu.CompilerParams(dimension_semantics=("arbitrary",),
                                             vmem_limit_bytes=VMEM_LIMIT_BYTES),
        name="prompt_layer",
    )(x, p, *params)


def _sample_call(x, p, sconv, spool, params, start):
    rows = x.shape[0]
    nb = sconv.shape[0]
    n = rows // nb
    vmem = pl.BlockSpec(memory_space=pltpu.VMEM)
    return pl.pallas_call(
        functools.partial(_sample_kernel, start),
        in_specs=[vmem] * (4 + len(params)),
        out_specs=[vmem] * 3,
        out_shape=[jax.ShapeDtypeStruct((rows, D_MODEL), F32),
                   jax.ShapeDtypeStruct((nb, CONV_HIST, CONV_WIDTH), F32),
                   jax.ShapeDtypeStruct((nb, POOL_HIST, POOL_WIDTH), F32)],
        scratch_shapes=[pltpu.VMEM((nb, U_SLOT + n, POOL_WIDTH), F32),
                        pltpu.VMEM((nb, CV_SLOT + n, CONV_WIDTH), F32),
                        pltpu.VMEM((rows, POOL_WIDTH), BF16),
                        pltpu.VMEM((rows, 2 * CONV_WIDTH), BF16)],
        compiler_params=pltpu.CompilerParams(vmem_limit_bytes=VMEM_LIMIT_BYTES),
        name="sample_layer",
    )(x, p, sconv, spool, *params)


def kernel(x_prompt, x_sample, state_conv, state_pool, p_prompt, p_sample, g_mix, w_in, w_conv,
           w_pool, pool_scale, w_out, g_mlp, w_up, w_down, g_ple, w_ple, w_ple_gate, g_final):
    depth = g_mix.shape[0]
    assert depth == 1, "single-layer trunk"
    batch, seq, _ = x_prompt.shape
    assert batch == 1
    dec_batch, dec_seq, _ = x_sample.shape
    past_len = 4096

    row = lambda v: v.reshape(1, -1)
    params = (row(g_mix[0]), w_in[0].astype(BF16), w_conv[0], w_pool[0].astype(BF16),
              row(pool_scale[0]), w_out[0].astype(BF16), row(g_mlp[0]), w_up[0].astype(BF16),
              w_down[0].astype(BF16), row(g_ple[0]), w_ple[0].astype(BF16),
              w_ple_gate[0].astype(BF16), row(g_final))

    yp, cp, pp = _prompt_call(x_prompt.reshape(seq, D_MODEL), p_prompt.reshape(seq, PLE_DIM),
                              params, PROMPT_TILE)
    ys, cs, ps = _sample_call(x_sample.reshape(dec_batch * dec_seq, D_MODEL),
                              p_sample.reshape(dec_batch * dec_seq, PLE_DIM),
                              state_conv[0], state_pool[0], params, past_len)
    return (yp.reshape(1, seq, D_MODEL), ys.reshape(dec_batch, dec_seq, D_MODEL),
            cp.reshape(1, 1, CONV_HIST, CONV_WIDTH), pp.reshape(1, 1, POOL_HIST, POOL_WIDTH),
            cs.reshape(1, dec_batch, CONV_HIST, CONV_WIDTH),
            ps.reshape(1, dec_batch, POOL_HIST, POOL_WIDTH))
```

```python
import collections
import functools

import jax
import jax.numpy as jnp
from jax import lax
from jax.experimental import pallas as pl
from jax.experimental.pallas import tpu as pltpu

D_MODEL = 1024
CONV_WIDTH = 512
CONV_HIST = 2
POOL_WIDTH = 512
POOL_WINDOWS = (2, 4, 8, 16)
POOL_GROUP = 128
POOL_HIST = 15
D_FF = 4096
PLE_DIM = 256
PAST_LEN = 4096
EPS = 1e-6

U_SLOT = 16
CV_SLOT = 8
FF_CHUNK = 1024
PROMPT_TILE = 512
PROMPT_SUB_TILE = 256
VMEM_LIMIT_BYTES = 56 * 1024 * 1024

F32 = jnp.float32
BF16 = jnp.bfloat16

_Weights = collections.namedtuple(
    "_Weights", "g_mix w_in w_conv w_pool pool_scale w_out g_mlp w_up w_down g_ple w_ple "
                "w_gate g_final")


def _rms(x, g):
    ms = jnp.mean(x * x, axis=-1, keepdims=True)
    return x * lax.rsqrt(ms + EPS) * g


def _dot(a, b):
    return jnp.dot(a, b, preferred_element_type=F32)


def _stage_project(x, w):
    h = _rms(x, w.g_mix[...]).astype(BF16)
    sec = lambda k: _dot(h, w.w_in[:, k * CONV_WIDTH:(k + 1) * CONV_WIDTH])
    zu, zc, zv, zb = sec(3), sec(1), sec(2), sec(0)
    return zb, zc, zv, zu


def _mix_segment(zb, zc, zv, zu, pos0, ubuf, cvbuf, dbuf, abbuf, w):
    n = zu.shape[0]
    ubuf[pl.ds(U_SLOT, n), :] = zu
    s = ubuf[pl.ds(0, U_SLOT + n), :]
    head = min(U_SLOT, n)
    pos = pos0 + lax.broadcasted_iota(jnp.int32, (head, POOL_GROUP), 0)
    for g, win in enumerate(POOL_WINDOWS):
        s = s + pltpu.roll(s, win // 2, axis=0)
        cols = slice(g * POOL_GROUP, (g + 1) * POOL_GROUP)
        sg = s[U_SLOT:, 0:POOL_GROUP]
        u = zu[:, cols]
        inv_head = 1.0 / jnp.minimum(pos + 1, win).astype(F32)
        dbuf[0:head, cols] = (sg[0:head] * inv_head - u[0:head]).astype(BF16)
        if n > head:
            dbuf[head:n, cols] = (sg[head:] * (1.0 / win) - u[head:]).astype(BF16)
        s = s[:, POOL_GROUP:]

    cv = zc * zv
    cvbuf[pl.ds(CV_SLOT, n), :] = cv
    conv = (w.w_conv[0:1, :] * cvbuf[pl.ds(CV_SLOT - 2, n), :]
            + w.w_conv[1:2, :] * cvbuf[pl.ds(CV_SLOT - 1, n), :]
            + w.w_conv[2:3, :] * cv)
    abbuf[:, 0:CONV_WIDTH] = (zb * conv).astype(BF16)


def _stage_out(x, dbuf, abbuf, w):
    for g in range(len(POOL_WINDOWS)):
        cols = slice(g * POOL_GROUP, (g + 1) * POOL_GROUP)
        y = _dot(dbuf[:, cols], w.w_pool[g]) * w.pool_scale[:, cols]
        abbuf[:, CONV_WIDTH + g * POOL_GROUP:CONV_WIDTH + (g + 1) * POOL_GROUP] = y.astype(BF16)
    return x + _dot(abbuf[...], w.w_out[...])


def _stage_ffn(x, w):
    h = _rms(x, w.g_mlp[...]).astype(BF16)
    for c in range(D_FF // FF_CHUNK):
        f = jnp.maximum(_dot(h, w.w_up[:, c * FF_CHUNK:(c + 1) * FF_CHUNK]), 0.0)
        x = x + _dot((f * f).astype(BF16), w.w_down[c * FF_CHUNK:(c + 1) * FF_CHUNK, :])
    return x


def _stage_gate(x, p, w):
    pe = _dot(p.astype(BF16), w.w_ple[...])
    h = _rms(x, w.g_ple[...]).astype(BF16)
    x = x + pe * jax.nn.sigmoid(_dot(h, w.w_gate[...]))
    return _rms(x, w.g_final[...])


def _prompt_kernel(sub, x_ref, p_ref, *refs):
    w = _Weights(*refs[:len(_Weights._fields)])
    y_ref, nconv_ref, npool_ref, ubuf, cvbuf, dbuf, abbuf = refs[len(_Weights._fields):]
    step = pl.program_id(0)
    n = x_ref.shape[0]
    subs = [pl.ds(k * sub, sub) for k in range(n // sub)]

    @pl.when(step == 0)
    def _():
        ubuf[0:U_SLOT, :] = jnp.zeros((U_SLOT, POOL_WIDTH), F32)
        cvbuf[0:CV_SLOT, :] = jnp.zeros((CV_SLOT, CONV_WIDTH), F32)

    zs = [_stage_project(x_ref[r, :], w) for r in subs]
    x1 = []
    for k, r in enumerate(subs):
        _mix_segment(*zs[k], step * n + k * sub, ubuf.at[pl.ds(k * sub, U_SLOT + sub), :],
                     cvbuf.at[pl.ds(k * sub, CV_SLOT + sub), :], dbuf.at[r, :], abbuf.at[r, :], w)
        x1.append(_stage_out(x_ref[r, :], dbuf.at[r, :], abbuf.at[r, :], w))
    x2 = [_stage_ffn(x, w) for x in x1]
    for k, r in enumerate(subs):
        y_ref[r, :] = _stage_gate(x2[k], p_ref[r, :], w)

    nconv_ref[...] = cvbuf[pl.ds(CV_SLOT + n - CONV_HIST, CONV_HIST), :]
    npool_ref[...] = ubuf[pl.ds(U_SLOT + n - POOL_HIST, POOL_HIST), :]
    ubuf[0:U_SLOT, :] = ubuf[pl.ds(n, U_SLOT), :]
    cvbuf[0:CV_SLOT, :] = cvbuf[pl.ds(n, CV_SLOT), :]


def _sample_kernel(start, x_ref, p_ref, sconv_ref, spool_ref, *refs):
    w = _Weights(*refs[:len(_Weights._fields)])
    y_ref, nconv_ref, npool_ref, ubuf, cvbuf, dbuf, abbuf = refs[len(_Weights._fields):]
    nb = sconv_ref.shape[0]
    n = x_ref.shape[0] // nb
    zb, zc, zv, zu = _stage_project(x_ref[...], w)
    for b in range(nb):
        rows = slice(b * n, (b + 1) * n)
        ub, cb = ubuf.at[b], cvbuf.at[b]
        ub[pl.ds(0, U_SLOT - POOL_HIST), :] = jnp.zeros((U_SLOT - POOL_HIST, POOL_WIDTH), F32)
        ub[pl.ds(U_SLOT - POOL_HIST, POOL_HIST), :] = spool_ref[b]
        cb[pl.ds(CV_SLOT - CONV_HIST, CONV_HIST), :] = sconv_ref[b]
        _mix_segment(zb[rows], zc[rows], zv[rows], zu[rows], start, ub, cb, dbuf.at[rows, :],
                     abbuf.at[rows, :], w)
        nconv_ref[b] = cb[pl.ds(CV_SLOT + n - CONV_HIST, CONV_HIST), :]
        npool_ref[b] = ub[pl.ds(U_SLOT + n - POOL_HIST, POOL_HIST), :]
    x = _stage_out(x_ref[...], dbuf, abbuf, w)
    y_ref[...] = _stage_gate(_stage_ffn(x, w), p_ref[...], w)


def _resident(shape):
    zeros = (0,) * len(shape)
    return pl.BlockSpec(shape, lambda i: zeros, pipeline_mode=pl.Buffered(1))


def _prompt_call(x, p, params, tile, sub):
    seq = x.shape[0]
    assert seq % tile == 0 and tile % sub == 0 and sub >= U_SLOT
    row_spec = lambda cols: pl.BlockSpec((tile, cols), lambda i: (i, 0))
    return pl.pallas_call(
        functools.partial(_prompt_kernel, sub),
        grid=(seq // tile,),
        in_specs=[row_spec(D_MODEL), row_spec(PLE_DIM)] + [_resident(v.shape) for v in params],
        out_specs=[row_spec(D_MODEL),
                   pl.BlockSpec((CONV_HIST, CONV_WIDTH), lambda i: (0, 0)),
                   pl.BlockSpec((POOL_HIST, POOL_WIDTH), lambda i: (0, 0))],
        out_shape=[jax.ShapeDtypeStruct((seq, D_MODEL), F32),
                   jax.ShapeDtypeStruct((CONV_HIST, CONV_WIDTH), F32),
                   jax.ShapeDtypeStruct((POOL_HIST, POOL_WIDTH), F32)],
        scratch_shapes=[pltpu.VMEM((U_SLOT + tile, POOL_WIDTH), F32),
                        pltpu.VMEM((CV_SLOT + tile, CONV_WIDTH), F32),
                        pltpu.VMEM((tile, POOL_WIDTH), BF16),
                        pltpu.VMEM((tile, 2 * CONV_WIDTH), BF16)],
        compiler_params=pltpu.CompilerParams(dimension_semantics=("arbitrary",),
                                             vmem_limit_bytes=VMEM_LIMIT_BYTES),
        name="prompt_layer",
    )(x, p, *params)


def _sample_call(x, p, sconv, spool, params, start):
    rows = x.shape[0]
    nb = sconv.shape[0]
    n = rows // nb
    vmem = pl.BlockSpec(memory_space=pltpu.VMEM)
    return pl.pallas_call(
        functools.partial(_sample_kernel, start),
        in_specs=[vmem] * (4 + len(params)),
        out_specs=[vmem] * 3,
        out_shape=[jax.ShapeDtypeStruct((rows, D_MODEL), F32),
                   jax.ShapeDtypeStruct((nb, CONV_HIST, CONV_WIDTH), F32),
                   jax.ShapeDtypeStruct((nb, POOL_HIST, POOL_WIDTH), F32)],
        scratch_shapes=[pltpu.VMEM((nb, U_SLOT + n, POOL_WIDTH), F32),
                        pltpu.VMEM((nb, CV_SLOT + n, CONV_WIDTH), F32),
                        pltpu.VMEM((rows, POOL_WIDTH), BF16),
                        pltpu.VMEM((rows, 2 * CONV_WIDTH), BF16)],
        compiler_params=pltpu.CompilerParams(vmem_limit_bytes=VMEM_LIMIT_BYTES),
        name="sample_layer",
    )(x, p, sconv, spool, *params)


def kernel(x_prompt, x_sample, state_conv, state_pool, p_prompt, p_sample, g_mix, w_in, w_conv,
           w_pool, pool_scale, w_out, g_mlp, w_up, w_down, g_ple, w_ple, w_ple_gate, g_final):
    assert g_mix.shape[0] == 1, "single-layer trunk"
    batch, seq, _ = x_prompt.shape
    assert batch == 1
    dec_batch, dec_seq, _ = x_sample.shape

    row = lambda v: v.reshape(1, -1)
    params = _Weights(
        g_mix=row(g_mix[0]), w_in=w_in[0].astype(BF16), w_conv=w_conv[0],
        w_pool=w_pool[0].astype(BF16), pool_scale=row(pool_scale[0]),
        w_out=w_out[0].astype(BF16), g_mlp=row(g_mlp[0]), w_up=w_up[0].astype(BF16),
        w_down=w_down[0].astype(BF16), g_ple=row(g_ple[0]), w_ple=w_ple[0].astype(BF16),
        w_gate=w_ple_gate[0].astype(BF16), g_final=row(g_final))

    yp, cp, pp = _prompt_call(x_prompt.reshape(seq, D_MODEL), p_prompt.reshape(seq, PLE_DIM),
                              params, PROMPT_TILE, PROMPT_SUB_TILE)
    ys, cs, ps = _sample_call(x_sample.reshape(dec_batch * dec_seq, D_MODEL),
                              p_sample.reshape(dec_batch * dec_seq, PLE_DIM),
                              state_conv[0], state_pool[0], params, PAST_LEN)
    return (yp.reshape(1, seq, D_MODEL), ys.reshape(dec_batch, dec_seq, D_MODEL),
            cp.reshape(1, 1, CONV_HIST, CONV_WIDTH), pp.reshape(1, 1, POOL_HIST, POOL_WIDTH),
            cs.reshape(1, dec_batch, CONV_HIST, CONV_WIDTH),
            ps.reshape(1, dec_batch, POOL_HIST, POOL_WIDTH))
```

```python
import collections
import functools

import jax
import jax.numpy as jnp
from jax import lax
from jax.experimental import pallas as pl
from jax.experimental.pallas import tpu as pltpu

D_MODEL = 1024
CONV_WIDTH = 512
CONV_HIST = 2
POOL_WIDTH = 512
POOL_WINDOWS = (2, 4, 8, 16)
POOL_GROUP = 128
POOL_HIST = 15
D_FF = 4096
PLE_DIM = 256
PAST_LEN = 4096
EPS = 1e-6

U_SLOT = 16
CV_SLOT = 8
FF_CHUNK = 1024
PROMPT_TILE = 512
PROMPT_SUB_TILE = 256
VMEM_LIMIT_BYTES = 56 * 1024 * 1024

F32 = jnp.float32
BF16 = jnp.bfloat16

_Weights = collections.namedtuple(
    "_Weights", "g_mix w_in w_conv w_pool pool_scale w_out g_mlp w_up w_down g_ple w_ple "
                "w_gate g_final")


def _rms(x, g):
    ms = jnp.mean(x * x, axis=-1, keepdims=True)
    return x * lax.rsqrt(ms + EPS) * g


def _dot(a, b):
    return jnp.dot(a, b, preferred_element_type=F32)


def _pack_rows(w):
    bits = lax.bitcast_convert_type(w.astype(BF16), jnp.uint16).astype(jnp.uint32)
    return bits[..., 0::2, :] | (bits[..., 1::2, :] << 16)


def _unpack(packed):
    return pltpu.bitcast(packed, BF16)


def _stage_project(x, w):
    h = _rms(x, w.g_mix[...]).astype(BF16)
    sec = lambda k: _dot(h, _unpack(w.w_in[:, k * CONV_WIDTH:(k + 1) * CONV_WIDTH]))
    zu, zc, zv, zb = sec(3), sec(1), sec(2), sec(0)
    return zb, zc, zv, zu


def _mix_segment(zb, zc, zv, zu, pos0, ubuf, cvbuf, dbuf, abbuf, w):
    n = zu.shape[0]
    ubuf[pl.ds(U_SLOT, n), :] = zu
    s = ubuf[pl.ds(0, U_SLOT + n), :]
    head = min(U_SLOT, n)
    pos = pos0 + lax.broadcasted_iota(jnp.int32, (head, POOL_GROUP), 0)
    for g, win in enumerate(POOL_WINDOWS):
        s = s + pltpu.roll(s, win // 2, axis=0)
        cols = slice(g * POOL_GROUP, (g + 1) * POOL_GROUP)
        sg = s[U_SLOT:, 0:POOL_GROUP]
        u = zu[:, cols]
        inv_head = 1.0 / jnp.minimum(pos + 1, win).astype(F32)
        dbuf[0:head, cols] = (sg[0:head] * inv_head - u[0:head]).astype(BF16)
        if n > head:
            dbuf[head:n, cols] = (sg[head:] * (1.0 / win) - u[head:]).astype(BF16)
        s = s[:, POOL_GROUP:]

    cv = zc * zv
    cvbuf[pl.ds(CV_SLOT, n), :] = cv
    conv = (w.w_conv[0:1, :] * cvbuf[pl.ds(CV_SLOT - 2, n), :]
            + w.w_conv[1:2, :] * cvbuf[pl.ds(CV_SLOT - 1, n), :]
            + w.w_conv[2:3, :] * cv)
    abbuf[:, 0:CONV_WIDTH] = (zb * conv).astype(BF16)


def _stage_out(x, dbuf, abbuf, w):
    for g in range(len(POOL_WINDOWS)):
        cols = slice(g * POOL_GROUP, (g + 1) * POOL_GROUP)
        y = _dot(dbuf[:, cols], _unpack(w.w_pool[g])) * w.pool_scale[:, cols]
        abbuf[:, CONV_WIDTH + g * POOL_GROUP:CONV_WIDTH + (g + 1) * POOL_GROUP] = y.astype(BF16)
    return x + _dot(abbuf[...], _unpack(w.w_out[...]))


def _stage_ffn(x, w):
    h = _rms(x, w.g_mlp[...]).astype(BF16)
    for c in range(D_FF // FF_CHUNK):
        f = jnp.maximum(_dot(h, _unpack(w.w_up[:, c * FF_CHUNK:(c + 1) * FF_CHUNK])), 0.0)
        w_down = _unpack(w.w_down[c * FF_CHUNK // 2:(c + 1) * FF_CHUNK // 2, :])
        x = x + _dot((f * f).astype(BF16), w_down)
    return x


def _stage_gate(x, p, w):
    pe = _dot(p.astype(BF16), _unpack(w.w_ple[...]))
    h = _rms(x, w.g_ple[...]).astype(BF16)
    x = x + pe * jax.nn.sigmoid(_dot(h, _unpack(w.w_gate[...])))
    return _rms(x, w.g_final[...])


def _prompt_kernel(sub, x_ref, p_ref, *refs):
    w = _Weights(*refs[:len(_Weights._fields)])
    y_ref, nconv_ref, npool_ref, ubuf, cvbuf, dbuf, abbuf = refs[len(_Weights._fields):]
    step = pl.program_id(0)
    n = x_ref.shape[0]
    subs = [pl.ds(k * sub, sub) for k in range(n // sub)]

    @pl.when(step == 0)
    def _():
        ubuf[0:U_SLOT, :] = jnp.zeros((U_SLOT, POOL_WIDTH), F32)
        cvbuf[0:CV_SLOT, :] = jnp.zeros((CV_SLOT, CONV_WIDTH), F32)

    zs = [_stage_project(x_ref[r, :], w) for r in subs]
    x1 = []
    for k, r in enumerate(subs):
        _mix_segment(*zs[k], step * n + k * sub, ubuf.at[pl.ds(k * sub, U_SLOT + sub), :],
                     cvbuf.at[pl.ds(k * sub, CV_SLOT + sub), :], dbuf.at[r, :], abbuf.at[r, :], w)
        x1.append(_stage_out(x_ref[r, :], dbuf.at[r, :], abbuf.at[r, :], w))
    x2 = [_stage_ffn(x, w) for x in x1]
    for k, r in enumerate(subs):
        y_ref[r, :] = _stage_gate(x2[k], p_ref[r, :], w)

    nconv_ref[...] = cvbuf[pl.ds(CV_SLOT + n - CONV_HIST, CONV_HIST), :]
    npool_ref[...] = ubuf[pl.ds(U_SLOT + n - POOL_HIST, POOL_HIST), :]
    ubuf[0:U_SLOT, :] = ubuf[pl.ds(n, U_SLOT), :]
    cvbuf[0:CV_SLOT, :] = cvbuf[pl.ds(n, CV_SLOT), :]


def _sample_kernel(start, x_ref, p_ref, sconv_ref, spool_ref, *refs):
    w = _Weights(*refs[:len(_Weights._fields)])
    y_ref, nconv_ref, npool_ref, ubuf, cvbuf, dbuf, abbuf = refs[len(_Weights._fields):]
    nb = sconv_ref.shape[0]
    n = x_ref.shape[0] // nb
    zb, zc, zv, zu = _stage_project(x_ref[...], w)
    for b in range(nb):
        rows = slice(b * n, (b + 1) * n)
        ub, cb = ubuf.at[b], cvbuf.at[b]
        ub[pl.ds(0, U_SLOT - POOL_HIST), :] = jnp.zeros((U_SLOT - POOL_HIST, POOL_WIDTH), F32)
        ub[pl.ds(U_SLOT - POOL_HIST, POOL_HIST), :] = spool_ref[b]
        cb[pl.ds(CV_SLOT - CONV_HIST, CONV_HIST), :] = sconv_ref[b]
        _mix_segment(zb[rows], zc[rows], zv[rows], zu[rows], start, ub, cb, dbuf.at[rows, :],
                     abbuf.at[rows, :], w)
        nconv_ref[b] = cb[pl.ds(CV_SLOT + n - CONV_HIST, CONV_HIST), :]
        npool_ref[b] = ub[pl.ds(U_SLOT + n - POOL_HIST, POOL_HIST), :]
    x = _stage_out(x_ref[...], dbuf, abbuf, w)
    y_ref[...] = _stage_gate(_stage_ffn(x, w), p_ref[...], w)


def _resident(shape):
    zeros = (0,) * len(shape)
    return pl.BlockSpec(shape, lambda i: zeros, pipeline_mode=pl.Buffered(1))


def _prompt_call(x, p, params, tile, sub):
    seq = x.shape[0]
    assert seq % tile == 0 and tile % sub == 0 and sub >= U_SLOT
    row_spec = lambda cols: pl.BlockSpec((tile, cols), lambda i: (i, 0))
    return pl.pallas_call(
        functools.partial(_prompt_kernel, sub),
        grid=(seq // tile,),
        in_specs=[row_spec(D_MODEL), row_spec(PLE_DIM)] + [_resident(v.shape) for v in params],
        out_specs=[row_spec(D_MODEL),
                   pl.BlockSpec((CONV_HIST, CONV_WIDTH), lambda i: (0, 0)),
                   pl.BlockSpec((POOL_HIST, POOL_WIDTH), lambda i: (0, 0))],
        out_shape=[jax.ShapeDtypeStruct((seq, D_MODEL), F32),
                   jax.ShapeDtypeStruct((CONV_HIST, CONV_WIDTH), F32),
                   jax.ShapeDtypeStruct((POOL_HIST, POOL_WIDTH), F32)],
        scratch_shapes=[pltpu.VMEM((U_SLOT + tile, POOL_WIDTH), F32),
                        pltpu.VMEM((CV_SLOT + tile, CONV_WIDTH), F32),
                        pltpu.VMEM((tile, POOL_WIDTH), BF16),
                        pltpu.VMEM((tile, 2 * CONV_WIDTH), BF16)],
        compiler_params=pltpu.CompilerParams(dimension_semantics=("arbitrary",),
                                             vmem_limit_bytes=VMEM_LIMIT_BYTES),
        name="prompt_layer",
    )(x, p, *params)


def _sample_call(x, p, sconv, spool, params, start):
    rows = x.shape[0]
    nb = sconv.shape[0]
    n = rows // nb
    vmem = pl.BlockSpec(memory_space=pltpu.VMEM)
    return pl.pallas_call(
        functools.partial(_sample_kernel, start),
        in_specs=[vmem] * (4 + len(params)),
        out_specs=[vmem] * 3,
        out_shape=[jax.ShapeDtypeStruct((rows, D_MODEL), F32),
                   jax.ShapeDtypeStruct((nb, CONV_HIST, CONV_WIDTH), F32),
                   jax.ShapeDtypeStruct((nb, POOL_HIST, POOL_WIDTH), F32)],
        scratch_shapes=[pltpu.VMEM((nb, U_SLOT + n, POOL_WIDTH), F32),
                        pltpu.VMEM((nb, CV_SLOT + n, CONV_WIDTH), F32),
                        pltpu.VMEM((rows, POOL_WIDTH), BF16),
                        pltpu.VMEM((rows, 2 * CONV_WIDTH), BF16)],
        compiler_params=pltpu.CompilerParams(vmem_limit_bytes=VMEM_LIMIT_BYTES),
        name="sample_layer",
    )(x, p, sconv, spool, *params)


def kernel(x_prompt, x_sample, state_conv, state_pool, p_prompt, p_sample, g_mix, w_in, w_conv,
           w_pool, pool_scale, w_out, g_mlp, w_up, w_down, g_ple, w_ple, w_ple_gate, g_final):
    assert g_mix.shape[0] == 1, "single-layer trunk"
    batch, seq, _ = x_prompt.shape
    assert batch == 1
    dec_batch, dec_seq, _ = x_sample.shape

    row = lambda v: v.reshape(1, -1)
    params = _Weights(
        g_mix=row(g_mix[0]), w_in=_pack_rows(w_in[0]), w_conv=w_conv[0],
        w_pool=_pack_rows(w_pool[0]), pool_scale=row(pool_scale[0]),
        w_out=_pack_rows(w_out[0]), g_mlp=row(g_mlp[0]), w_up=_pack_rows(w_up[0]),
        w_down=_pack_rows(w_down[0]), g_ple=row(g_ple[0]), w_ple=_pack_rows(w_ple[0]),
        w_gate=_pack_rows(w_ple_gate[0]), g_final=row(g_final))

    yp, cp, pp = _prompt_call(x_prompt.reshape(seq, D_MODEL), p_prompt.reshape(seq, PLE_DIM),
                              params, PROMPT_TILE, PROMPT_SUB_TILE)
    ys, cs, ps = _sample_call(x_sample.reshape(dec_batch * dec_seq, D_MODEL),
                              p_sample.reshape(dec_batch * dec_seq, PLE_DIM),
                              state_conv[0], state_pool[0], params, PAST_LEN)
    return (yp.reshape(1, seq, D_MODEL), ys.reshape(dec_batch, dec_seq, D_MODEL),
            cp.reshape(1, 1, CONV_HIST, CONV_WIDTH), pp.reshape(1, 1, POOL_HIST, POOL_WIDTH),
            cs.reshape(1, dec_batch, CONV_HIST, CONV_WIDTH),
            ps.reshape(1, dec_batch, POOL_HIST, POOL_WIDTH))
```

```python
import collections
import functools

import jax
import jax.numpy as jnp
from jax import lax
from jax.experimental import pallas as pl
from jax.experimental.pallas import tpu as pltpu

D_MODEL = 1024
CONV_WIDTH = 512
CONV_HIST = 2
POOL_WIDTH = 512
POOL_WINDOWS = (2, 4, 8, 16)
POOL_GROUP = 128
POOL_HIST = 15
D_FF = 4096
PLE_DIM = 256
PAST_LEN = 4096
EPS = 1e-6

U_SLOT = 16
CV_SLOT = 8
FF_CHUNK = 1024
PROMPT_TILE = 512
PROMPT_SUB_TILE = 256
STAGE_ROWS = 256
STAGE_COLS = 1024
STAGE_SLOTS = 4
VMEM_LIMIT_BYTES = 56 * 1024 * 1024

F32 = jnp.float32
BF16 = jnp.bfloat16

_STREAMED = ("w_in", "w_out", "w_up", "w_down", "w_ple", "w_gate")
_Weights = collections.namedtuple(
    "_Weights", "g_mix w_conv pool_scale g_mlp g_ple g_final w_pool " + " ".join(_STREAMED))
_Mixer = collections.namedtuple("_Mixer", "ubuf cvbuf dbuf abbuf")


def _rms(x, g):
    ms = jnp.mean(x * x, axis=-1, keepdims=True)
    return x * lax.rsqrt(ms + EPS) * g


def _dot(a, b):
    return jnp.dot(a, b, preferred_element_type=F32)


def _pack(w):
    return pltpu.bitcast(w.astype(BF16), jnp.uint32)


def _unpack(packed):
    return pltpu.bitcast(packed, BF16)


def _load_weights(hbm, resident, stage, sems):
    jobs = []
    for src, dst in zip(hbm, resident):
        rows, cols = src.shape
        assert rows % STAGE_ROWS == 0 and cols % STAGE_COLS == 0
        for r in range(0, rows, STAGE_ROWS):
            for c in range(0, cols, STAGE_COLS):
                jobs.append((src.at[pl.ds(r, STAGE_ROWS), pl.ds(c, STAGE_COLS)],
                             dst.at[pl.ds(r // 2, STAGE_ROWS // 2), pl.ds(c, STAGE_COLS)]))

    def copy(j):
        slot = j % STAGE_SLOTS
        return pltpu.make_async_copy(jobs[j][0], stage.at[slot], sems.at[slot])

    for j in range(min(STAGE_SLOTS, len(jobs))):
        copy(j).start()
    for j in range(len(jobs)):
        copy(j).wait()
        jobs[j][1][...] = _pack(stage[j % STAGE_SLOTS])
        if j + STAGE_SLOTS < len(jobs):
            copy(j + STAGE_SLOTS).start()


def _stage_project(x, w):
    h = _rms(x, w.g_mix[...]).astype(BF16)
    sec = lambda k: _dot(h, _unpack(w.w_in[:, k * CONV_WIDTH:(k + 1) * CONV_WIDTH]))
    zu, zc, zv, zb = sec(3), sec(1), sec(2), sec(0)
    return zb, zc, zv, zu


def _mix_segment(zb, zc, zv, zu, pos0, m, w):
    n = zu.shape[0]
    m.ubuf[pl.ds(U_SLOT, n), :] = zu
    s = m.ubuf[pl.ds(0, U_SLOT + n), :]
    head = min(U_SLOT, n)
    pos = pos0 + lax.broadcasted_iota(jnp.int32, (head, POOL_GROUP), 0)
    for g, win in enumerate(POOL_WINDOWS):
        s = s + pltpu.roll(s, win // 2, axis=0)
        cols = slice(g * POOL_GROUP, (g + 1) * POOL_GROUP)
        sg = s[U_SLOT:, 0:POOL_GROUP]
        u = zu[:, cols]
        inv_head = 1.0 / jnp.minimum(pos + 1, win).astype(F32)
        m.dbuf[0:head, cols] = (sg[0:head] * inv_head - u[0:head]).astype(BF16)
        if n > head:
            m.dbuf[head:n, cols] = (sg[head:] * (1.0 / win) - u[head:]).astype(BF16)
        s = s[:, POOL_GROUP:]

    cv = zc * zv
    m.cvbuf[pl.ds(CV_SLOT, n), :] = cv
    conv = (w.w_conv[0:1, :] * m.cvbuf[pl.ds(CV_SLOT - 2, n), :]
            + w.w_conv[1:2, :] * m.cvbuf[pl.ds(CV_SLOT - 1, n), :]
            + w.w_conv[2:3, :] * cv)
    m.abbuf[:, 0:CONV_WIDTH] = (zb * conv).astype(BF16)


def _stage_out(x, m, w):
    for g in range(len(POOL_WINDOWS)):
        cols = slice(g * POOL_GROUP, (g + 1) * POOL_GROUP)
        y = _dot(m.dbuf[:, cols], _unpack(w.w_pool[g])) * w.pool_scale[:, cols]
        m.abbuf[:, CONV_WIDTH + g * POOL_GROUP:CONV_WIDTH + (g + 1) * POOL_GROUP] = y.astype(BF16)
    return x + _dot(m.abbuf[...], _unpack(w.w_out[...]))


def _stage_ffn(x, w):
    h = _rms(x, w.g_mlp[...]).astype(BF16)
    for c in range(D_FF // FF_CHUNK):
        f = jnp.maximum(_dot(h, _unpack(w.w_up[:, c * FF_CHUNK:(c + 1) * FF_CHUNK])), 0.0)
        w_down = _unpack(w.w_down[c * FF_CHUNK // 2:(c + 1) * FF_CHUNK // 2, :])
        x = x + _dot((f * f).astype(BF16), w_down)
    return x


def _stage_gate(x, p, w):
    pe = _dot(p.astype(BF16), _unpack(w.w_ple[...]))
    h = _rms(x, w.g_ple[...]).astype(BF16)
    x = x + pe * jax.nn.sigmoid(_dot(h, _unpack(w.w_gate[...])))
    return _rms(x, w.g_final[...])


def _prompt_tile(step, sub, x_ref, p_ref, y_ref, nconv_ref, npool_ref, m, w):
    n = x_ref.shape[0]
    subs = [pl.ds(k * sub, sub) for k in range(n // sub)]
    zs = [_stage_project(x_ref[r, :], w) for r in subs]
    x1 = []
    for k, r in enumerate(subs):
        mk = _Mixer(m.ubuf.at[pl.ds(k * sub, U_SLOT + sub), :],
                    m.cvbuf.at[pl.ds(k * sub, CV_SLOT + sub), :], m.dbuf.at[r, :], m.abbuf.at[r, :])
        _mix_segment(*zs[k], step * n + k * sub, mk, w)
        x1.append(_stage_out(x_ref[r, :], mk, w))
    x2 = [_stage_ffn(x, w) for x in x1]
    for k, r in enumerate(subs):
        y_ref[r, :] = _stage_gate(x2[k], p_ref[r, :], w)

    nconv_ref[...] = m.cvbuf[pl.ds(CV_SLOT + n - CONV_HIST, CONV_HIST), :]
    npool_ref[...] = m.ubuf[pl.ds(U_SLOT + n - POOL_HIST, POOL_HIST), :]
    m.ubuf[0:U_SLOT, :] = m.ubuf[pl.ds(n, U_SLOT), :]
    m.cvbuf[0:CV_SLOT, :] = m.cvbuf[pl.ds(n, CV_SLOT), :]


def _sample_rows(start, x_ref, p_ref, sconv_ref, spool_ref, y_ref, nconv_ref, npool_ref, m, w):
    nb = sconv_ref.shape[0]
    n = x_ref.shape[0] // nb
    zb, zc, zv, zu = _stage_project(x_ref[...], w)
    for b in range(nb):
        rows = slice(b * n, (b + 1) * n)
        mb = _Mixer(m.ubuf.at[b], m.cvbuf.at[b], m.dbuf.at[rows, :], m.abbuf.at[rows, :])
        mb.ubuf[pl.ds(0, U_SLOT - POOL_HIST), :] = jnp.zeros((U_SLOT - POOL_HIST, POOL_WIDTH), F32)
        mb.ubuf[pl.ds(U_SLOT - POOL_HIST, POOL_HIST), :] = spool_ref[b]
        mb.cvbuf[pl.ds(CV_SLOT - CONV_HIST, CONV_HIST), :] = sconv_ref[b]
        _mix_segment(zb[rows], zc[rows], zv[rows], zu[rows], start, mb, w)
        nconv_ref[b] = mb.cvbuf[pl.ds(CV_SLOT + n - CONV_HIST, CONV_HIST), :]
        npool_ref[b] = mb.ubuf[pl.ds(U_SLOT + n - POOL_HIST, POOL_HIST), :]
    x = _stage_out(x_ref[...], m, w)
    y_ref[...] = _stage_gate(_stage_ffn(x, w), p_ref[...], w)


def _layer_kernel(tiles, sub, start, *refs):
    refs = list(refs)
    take = lambda k: [refs.pop(0) for _ in range(k)]
    x_ref, p_ref, xs_ref, ps_ref, sconv_ref, spool_ref = take(6)
    g_mix, w_conv, pool_scale, g_mlp, g_ple, g_final, w_pool_f32 = take(7)
    hbm = take(len(_STREAMED))
    y_ref, nconv_ref, npool_ref, ys_ref, nconv_s_ref, npool_s_ref = take(6)
    resident = take(len(_STREAMED))
    w_pool, stage, sems = take(3)
    m_prompt = _Mixer(*take(4))
    m_sample = _Mixer(*take(4))
    assert not refs
    w = _Weights(g_mix, w_conv, pool_scale, g_mlp, g_ple, g_final, w_pool, *resident)
    step = pl.program_id(0)

    @pl.when(step == 0)
    def _():
        _load_weights(hbm, resident, stage, sems)
        for g in range(len(POOL_WINDOWS)):
            w_pool[g] = _pack(w_pool_f32[g])
        m_prompt.ubuf[0:U_SLOT, :] = jnp.zeros((U_SLOT, POOL_WIDTH), F32)
        m_prompt.cvbuf[0:CV_SLOT, :] = jnp.zeros((CV_SLOT, CONV_WIDTH), F32)

    @pl.when(step < tiles)
    def _():
        _prompt_tile(step, sub, x_ref, p_ref, y_ref, nconv_ref, npool_ref, m_prompt, w)

    @pl.when(step == tiles)
    def _():
        _sample_rows(start, xs_ref, ps_ref, sconv_ref, spool_ref, ys_ref, nconv_s_ref, npool_s_ref,
                     m_sample, w)


def _resident(shape):
    zeros = (0,) * len(shape)
    return pl.BlockSpec(shape, lambda i: zeros, pipeline_mode=pl.Buffered(1))


def _whole(shape):
    zeros = (0,) * len(shape)
    return pl.BlockSpec(shape, lambda i: zeros)


def _layer_call(x, p, xs, ps, sconv, spool, small, streamed, tile, sub, start):
    seq, rows = x.shape[0], xs.shape[0]
    nb = sconv.shape[0]
    n = rows // nb
    assert seq % tile == 0 and tile % sub == 0 and sub >= U_SLOT
    tiles = seq // tile
    row_spec = lambda cols: pl.BlockSpec((tile, cols), lambda i: (jnp.minimum(i, tiles - 1), 0))
    f32_out = lambda *shape: jax.ShapeDtypeStruct(shape, F32)
    mixer_scratch = lambda lead, r: [pltpu.VMEM(lead + (U_SLOT + r, POOL_WIDTH), F32),
                                     pltpu.VMEM(lead + (CV_SLOT + r, CONV_WIDTH), F32)]
    return pl.pallas_call(
        functools.partial(_layer_kernel, tiles, sub, start),
        grid=(tiles + 1,),
        in_specs=([row_spec(D_MODEL), row_spec(PLE_DIM)]
                  + [_resident(v.shape) for v in (xs, ps, sconv, spool) + tuple(small)]
                  + [pl.BlockSpec(memory_space=pl.ANY)] * len(streamed)),
        out_specs=[row_spec(D_MODEL), _whole((CONV_HIST, CONV_WIDTH)),
                   _whole((POOL_HIST, POOL_WIDTH)), _whole((rows, D_MODEL)),
                   _whole((nb, CONV_HIST, CONV_WIDTH)), _whole((nb, POOL_HIST, POOL_WIDTH))],
        out_shape=[f32_out(seq, D_MODEL), f32_out(CONV_HIST, CONV_WIDTH),
                   f32_out(POOL_HIST, POOL_WIDTH), f32_out(rows, D_MODEL),
                   f32_out(nb, CONV_HIST, CONV_WIDTH), f32_out(nb, POOL_HIST, POOL_WIDTH)],
        scratch_shapes=([pltpu.VMEM((v.shape[0] // 2, v.shape[1]), jnp.uint32) for v in streamed]
                        + [pltpu.VMEM((len(POOL_WINDOWS), POOL_GROUP // 2, POOL_GROUP),
                                      jnp.uint32),
                           pltpu.VMEM((STAGE_SLOTS, STAGE_ROWS, STAGE_COLS), F32),
                           pltpu.SemaphoreType.DMA((STAGE_SLOTS,))]
                        + mixer_scratch((), tile)
                        + [pltpu.VMEM((tile, POOL_WIDTH), BF16),
                           pltpu.VMEM((tile, 2 * CONV_WIDTH), BF16)]
                        + mixer_scratch((nb,), n)
                        + [pltpu.VMEM((rows, POOL_WIDTH), BF16),
                           pltpu.VMEM((rows, 2 * CONV_WIDTH), BF16)]),
        compiler_params=pltpu.CompilerParams(dimension_semantics=("arbitrary",),
                                             vmem_limit_bytes=VMEM_LIMIT_BYTES),
        name="encoder_layer",
    )(x, p, xs, ps, sconv, spool, *small, *streamed)


def kernel(x_prompt, x_sample, state_conv, state_pool, p_prompt, p_sample, g_mix, w_in, w_conv,
           w_pool, pool_scale, w_out, g_mlp, w_up, w_down, g_ple, w_ple, w_ple_gate, g_final):
    assert g_mix.shape[0] == 1, "single-layer trunk"
    batch, seq, _ = x_prompt.shape
    assert batch == 1
    dec_batch, dec_seq, _ = x_sample.shape

    row = lambda v: v.reshape(1, -1)
    small = (row(g_mix[0]), w_conv[0], row(pool_scale[0]), row(g_mlp[0]), row(g_ple[0]),
             row(g_final), w_pool[0])
    streamed = (w_in[0], w_out[0], w_up[0], w_down[0], w_ple[0], w_ple_gate[0])

    yp, cp, pp, ys, cs, ps = _layer_call(
        x_prompt.reshape(seq, D_MODEL), p_prompt.reshape(seq, PLE_DIM),
        x_sample.reshape(dec_batch * dec_seq, D_MODEL),
        p_sample.reshape(dec_batch * dec_seq, PLE_DIM), state_conv[0], state_pool[0],
        small, streamed, PROMPT_TILE, PROMPT_SUB_TILE, PAST_LEN)
    return (yp.reshape(1, seq, D_MODEL), ys.reshape(dec_batch, dec_seq, D_MODEL),
            cp.reshape(1, 1, CONV_HIST, CONV_WIDTH), pp.reshape(1, 1, POOL_HIST, POOL_WIDTH),
            cs.reshape(1, dec_batch, CONV_HIST, CONV_WIDTH),
            ps.reshape(1, dec_batch, POOL_HIST, POOL_WIDTH))
```

```python
import collections
import functools

import jax
import jax.numpy as jnp
from jax import lax
from jax.experimental import pallas as pl
from jax.experimental.pallas import tpu as pltpu

D_MODEL = 1024
CONV_WIDTH = 512
CONV_HIST = 2
POOL_WIDTH = 512
POOL_WINDOWS = (2, 4, 8, 16)
POOL_GROUP = 128
POOL_HIST = 15
D_FF = 4096
PLE_DIM = 256
PAST_LEN = 4096
EPS = 1e-6

U_SLOT = 16
CV_SLOT = 8
FF_CHUNK = 1024
PROMPT_TILE = 512
PROMPT_SUB_TILE = 256
STAGE_ROWS = 256
STAGE_COLS = 1024
STAGE_SLOTS = 4
VMEM_LIMIT_BYTES = 56 * 1024 * 1024

F32 = jnp.float32
BF16 = jnp.bfloat16

_STREAMED = ("w_in", "w_out", "w_up", "w_down", "w_ple", "w_gate")
_Weights = collections.namedtuple(
    "_Weights", "g_mix w_conv pool_scale g_mlp g_ple g_final w_pool " + " ".join(_STREAMED))
_Mixer = collections.namedtuple("_Mixer", "ubuf cvbuf dbuf abbuf")
_Carry = collections.namedtuple("_Carry", "h zu z3 x")


def _rms(x, g):
    ms = jnp.mean(x * x, axis=-1, keepdims=True)
    return x * lax.rsqrt(ms + EPS) * g


def _dot(a, b):
    return jnp.dot(a, b, preferred_element_type=F32)


def _pack(w):
    return pltpu.bitcast(w.astype(BF16), jnp.uint32)


def _unpack(packed):
    return pltpu.bitcast(packed, BF16)


def _load_weights(hbm, resident, stage, sems):
    jobs = []
    for src, dst in zip(hbm, resident):
        rows, cols = src.shape
        assert rows % STAGE_ROWS == 0 and cols % STAGE_COLS == 0
        for r in range(0, rows, STAGE_ROWS):
            for c in range(0, cols, STAGE_COLS):
                jobs.append((src.at[pl.ds(r, STAGE_ROWS), pl.ds(c, STAGE_COLS)],
                             dst.at[pl.ds(r // 2, STAGE_ROWS // 2), pl.ds(c, STAGE_COLS)]))

    def copy(j):
        slot = j % STAGE_SLOTS
        return pltpu.make_async_copy(jobs[j][0], stage.at[slot], sems.at[slot])

    for j in range(min(STAGE_SLOTS, len(jobs))):
        copy(j).start()
    for j in range(len(jobs)):
        copy(j).wait()
        jobs[j][1][...] = _pack(stage[j % STAGE_SLOTS])
        if j + STAGE_SLOTS < len(jobs):
            copy(j + STAGE_SLOTS).start()


def _norm_in(x, w):
    return _rms(x, w.g_mix[...]).astype(BF16)


def _project(h, w, sections):
    cols = {"b": 0, "c": 1, "v": 2, "u": 3}
    return {s: _dot(h, _unpack(w.w_in[:, cols[s] * CONV_WIDTH:(cols[s] + 1) * CONV_WIDTH]))
            for s in sections}


def _mix_segment(zb, zc, zv, zu, pos0, m, w):
    n = zu.shape[0]
    m.ubuf[pl.ds(U_SLOT, n), :] = zu
    s = m.ubuf[pl.ds(0, U_SLOT + n), :]
    head = min(U_SLOT, n)
    pos = pos0 + lax.broadcasted_iota(jnp.int32, (head, POOL_GROUP), 0)
    for g, win in enumerate(POOL_WINDOWS):
        s = s + pltpu.roll(s, win // 2, axis=0)
        cols = slice(g * POOL_GROUP, (g + 1) * POOL_GROUP)
        sg = s[U_SLOT:, 0:POOL_GROUP]
        u = zu[:, cols]
        inv_head = 1.0 / jnp.minimum(pos + 1, win).astype(F32)
        m.dbuf[0:head, cols] = (sg[0:head] * inv_head - u[0:head]).astype(BF16)
        if n > head:
            m.dbuf[head:n, cols] = (sg[head:] * (1.0 / win) - u[head:]).astype(BF16)
        s = s[:, POOL_GROUP:]

    cv = zc * zv
    m.cvbuf[pl.ds(CV_SLOT, n), :] = cv
    conv = (w.w_conv[0:1, :] * m.cvbuf[pl.ds(CV_SLOT - 2, n), :]
            + w.w_conv[1:2, :] * m.cvbuf[pl.ds(CV_SLOT - 1, n), :]
            + w.w_conv[2:3, :] * cv)
    m.abbuf[:, 0:CONV_WIDTH] = (zb * conv).astype(BF16)


def _stage_out(x, m, w):
    for g in range(len(POOL_WINDOWS)):
        cols = slice(g * POOL_GROUP, (g + 1) * POOL_GROUP)
        y = _dot(m.dbuf[:, cols], _unpack(w.w_pool[g])) * w.pool_scale[:, cols]
        m.abbuf[:, CONV_WIDTH + g * POOL_GROUP:CONV_WIDTH + (g + 1) * POOL_GROUP] = y.astype(BF16)
    return x + _dot(m.abbuf[...], _unpack(w.w_out[...]))


def _stage_ffn(x, w):
    h = _rms(x, w.g_mlp[...]).astype(BF16)
    for c in range(D_FF // FF_CHUNK):
        f = jnp.maximum(_dot(h, _unpack(w.w_up[:, c * FF_CHUNK:(c + 1) * FF_CHUNK])), 0.0)
        w_down = _unpack(w.w_down[c * FF_CHUNK // 2:(c + 1) * FF_CHUNK // 2, :])
        x = x + _dot((f * f).astype(BF16), w_down)
    return x


def _stage_gate(x, p, w):
    pe = _dot(p.astype(BF16), _unpack(w.w_ple[...]))
    h = _rms(x, w.g_ple[...]).astype(BF16)
    x = x + pe * jax.nn.sigmoid(_dot(h, _unpack(w.w_gate[...])))
    return _rms(x, w.g_final[...])


def _stage_ahead_early(tile, x_ref, c, m, w):
    n = x_ref.shape[0]
    sub = n // 2
    z = _project(_norm_in(x_ref[0:sub, :], w), w, "ucvb")
    h = _norm_in(x_ref[sub:n, :], w)
    c.h[...] = pltpu.bitcast(h, jnp.uint32)
    c.zu[...] = _project(h, w, "u")["u"]
    _mix_segment(z["b"], z["c"], z["v"], z["u"], tile * n, _sub_mixer(m, 0, sub), w)


def _stage_ahead_late(x_ref, c, w):
    z = _project(_unpack(c.h[...]), w, "cvb")
    for k, s in enumerate("cvb"):
        c.z3[k] = z[s]
    c.x[...] = x_ref[...]


def _sub_mixer(m, k, sub):
    r = pl.ds(k * sub, sub)
    return _Mixer(m.ubuf.at[pl.ds(k * sub, U_SLOT + sub), :],
                  m.cvbuf.at[pl.ds(k * sub, CV_SLOT + sub), :], m.dbuf.at[r, :], m.abbuf.at[r, :])


def _prompt_tile(tile, x_ref, p_ref, y_ref, nconv_ref, npool_ref, c, m, w):
    n = c.x.shape[0]
    sub = n // 2
    subs = [pl.ds(k * sub, sub) for k in range(2)]
    _mix_segment(c.z3[2], c.z3[0], c.z3[1], c.zu[...], tile * n + sub, _sub_mixer(m, 1, sub), w)
    nconv_ref[...] = m.cvbuf[pl.ds(CV_SLOT + n - CONV_HIST, CONV_HIST), :]
    npool_ref[...] = m.ubuf[pl.ds(U_SLOT + n - POOL_HIST, POOL_HIST), :]
    m.ubuf[0:U_SLOT, :] = m.ubuf[pl.ds(n, U_SLOT), :]
    m.cvbuf[0:CV_SLOT, :] = m.cvbuf[pl.ds(n, CV_SLOT), :]

    x1 = [_stage_out(c.x[r, :], _sub_mixer(m, k, sub), w) for k, r in enumerate(subs)]
    x2a = _stage_ffn(x1[0], w)
    _stage_ahead_early(tile + 1, x_ref, c, m, w)
    x2b = _stage_ffn(x1[1], w)
    for r, x2 in zip(subs, (x2a, x2b)):
        y_ref[r, :] = _stage_gate(x2, p_ref[r, :], w)
    _stage_ahead_late(x_ref, c, w)


def _sample_rows(start, x_ref, p_ref, sconv_ref, spool_ref, y_ref, nconv_ref, npool_ref, m, w):
    nb = sconv_ref.shape[0]
    n = x_ref.shape[0] // nb
    z = _project(_norm_in(x_ref[...], w), w, "ucvb")
    zb, zc, zv, zu = z["b"], z["c"], z["v"], z["u"]
    for b in range(nb):
        rows = slice(b * n, (b + 1) * n)
        mb = _Mixer(m.ubuf.at[b], m.cvbuf.at[b], m.dbuf.at[rows, :], m.abbuf.at[rows, :])
        mb.ubuf[pl.ds(0, U_SLOT - POOL_HIST), :] = jnp.zeros((U_SLOT - POOL_HIST, POOL_WIDTH), F32)
        mb.ubuf[pl.ds(U_SLOT - POOL_HIST, POOL_HIST), :] = spool_ref[b]
        mb.cvbuf[pl.ds(CV_SLOT - CONV_HIST, CONV_HIST), :] = sconv_ref[b]
        _mix_segment(zb[rows], zc[rows], zv[rows], zu[rows], start, mb, w)
        nconv_ref[b] = mb.cvbuf[pl.ds(CV_SLOT + n - CONV_HIST, CONV_HIST), :]
        npool_ref[b] = mb.ubuf[pl.ds(U_SLOT + n - POOL_HIST, POOL_HIST), :]
    x = _stage_out(x_ref[...], m, w)
    y_ref[...] = _stage_gate(_stage_ffn(x, w), p_ref[...], w)


def _layer_kernel(tiles, sub, start, *refs):
    refs = list(refs)
    take = lambda k: [refs.pop(0) for _ in range(k)]
    x_ref, p_ref, xs_ref, ps_ref, sconv_ref, spool_ref = take(6)
    g_mix, w_conv, pool_scale, g_mlp, g_ple, g_final, w_pool_f32 = take(7)
    hbm = take(len(_STREAMED))
    y_ref, nconv_ref, npool_ref, ys_ref, nconv_s_ref, npool_s_ref = take(6)
    resident = take(len(_STREAMED))
    w_pool, stage, sems = take(3)
    m_prompt = _Mixer(*take(4))
    m_sample = _Mixer(*take(4))
    carry = _Carry(*take(4))
    assert not refs
    w = _Weights(g_mix, w_conv, pool_scale, g_mlp, g_ple, g_final, w_pool, *resident)
    step = pl.program_id(0)

    @pl.when(step == 0)
    def _():
        _load_weights(hbm, resident, stage, sems)
        for g in range(len(POOL_WINDOWS)):
            w_pool[g] = _pack(w_pool_f32[g])
        m_prompt.ubuf[0:U_SLOT, :] = jnp.zeros((U_SLOT, POOL_WIDTH), F32)
        m_prompt.cvbuf[0:CV_SLOT, :] = jnp.zeros((CV_SLOT, CONV_WIDTH), F32)
        _stage_ahead_early(0, x_ref, carry, m_prompt, w)
        _stage_ahead_late(x_ref, carry, w)

    @pl.when(jnp.logical_and(step >= 1, step <= tiles))
    def _():
        _prompt_tile(step - 1, x_ref, p_ref, y_ref, nconv_ref, npool_ref, carry, m_prompt, w)

    @pl.when(step == tiles + 1)
    def _():
        _sample_rows(start, xs_ref, ps_ref, sconv_ref, spool_ref, ys_ref, nconv_s_ref, npool_s_ref,
                     m_sample, w)


def _resident(shape):
    zeros = (0,) * len(shape)
    return pl.BlockSpec(shape, lambda i: zeros, pipeline_mode=pl.Buffered(1))


def _whole(shape):
    zeros = (0,) * len(shape)
    return pl.BlockSpec(shape, lambda i: zeros)


def _layer_call(x, p, xs, ps, sconv, spool, small, streamed, tile, sub, start):
    seq, rows = x.shape[0], xs.shape[0]
    nb = sconv.shape[0]
    n = rows // nb
    assert seq % tile == 0 and tile == 2 * sub and sub >= U_SLOT
    tiles = seq // tile
    ahead_spec = pl.BlockSpec((tile, D_MODEL), lambda i: (jnp.minimum(i, tiles - 1), 0))
    row_spec = lambda cols: pl.BlockSpec((tile, cols), lambda i: (jnp.clip(i - 1, 0, tiles - 1), 0))
    f32_out = lambda *shape: jax.ShapeDtypeStruct(shape, F32)
    mixer_scratch = lambda lead, r: [pltpu.VMEM(lead + (U_SLOT + r, POOL_WIDTH), F32),
                                     pltpu.VMEM(lead + (CV_SLOT + r, CONV_WIDTH), F32)]
    return pl.pallas_call(
        functools.partial(_layer_kernel, tiles, sub, start),
        grid=(tiles + 2,),
        in_specs=([ahead_spec, row_spec(PLE_DIM)]
                  + [_resident(v.shape) for v in (xs, ps, sconv, spool) + tuple(small)]
                  + [pl.BlockSpec(memory_space=pl.ANY)] * len(streamed)),
        out_specs=[row_spec(D_MODEL), _whole((CONV_HIST, CONV_WIDTH)),
                   _whole((POOL_HIST, POOL_WIDTH)), _whole((rows, D_MODEL)),
                   _whole((nb, CONV_HIST, CONV_WIDTH)), _whole((nb, POOL_HIST, POOL_WIDTH))],
        out_shape=[f32_out(seq, D_MODEL), f32_out(CONV_HIST, CONV_WIDTH),
                   f32_out(POOL_HIST, POOL_WIDTH), f32_out(rows, D_MODEL),
                   f32_out(nb, CONV_HIST, CONV_WIDTH), f32_out(nb, POOL_HIST, POOL_WIDTH)],
        scratch_shapes=([pltpu.VMEM((v.shape[0] // 2, v.shape[1]), jnp.uint32) for v in streamed]
                        + [pltpu.VMEM((len(POOL_WINDOWS), POOL_GROUP // 2, POOL_GROUP),
                                      jnp.uint32),
                           pltpu.VMEM((STAGE_SLOTS, STAGE_ROWS, STAGE_COLS), F32),
                           pltpu.SemaphoreType.DMA((STAGE_SLOTS,))]
                        + mixer_scratch((), tile)
                        + [pltpu.VMEM((tile, POOL_WIDTH), BF16),
                           pltpu.VMEM((tile, 2 * CONV_WIDTH), BF16)]
                        + mixer_scratch((nb,), n)
                        + [pltpu.VMEM((rows, POOL_WIDTH), BF16),
                           pltpu.VMEM((rows, 2 * CONV_WIDTH), BF16),
                           pltpu.VMEM((sub // 2, D_MODEL), jnp.uint32),
                           pltpu.VMEM((sub, POOL_WIDTH), F32),
                           pltpu.VMEM((3, sub, CONV_WIDTH), F32),
                           pltpu.VMEM((tile, D_MODEL), F32)]),
        compiler_params=pltpu.CompilerParams(dimension_semantics=("arbitrary",),
                                             vmem_limit_bytes=VMEM_LIMIT_BYTES),
        name="encoder_layer",
    )(x, p, xs, ps, sconv, spool, *small, *streamed)


def kernel(x_prompt, x_sample, state_conv, state_pool, p_prompt, p_sample, g_mix, w_in, w_conv,
           w_pool, pool_scale, w_out, g_mlp, w_up, w_down, g_ple, w_ple, w_ple_gate, g_final):
    assert g_mix.shape[0] == 1, "single-layer trunk"
    batch, seq, _ = x_prompt.shape
    assert batch == 1
    dec_batch, dec_seq, _ = x_sample.shape

    drop = lambda v: v.reshape(v.shape[1:])
    small = (g_mix, drop(w_conv), pool_scale, g_mlp, g_ple, g_final.reshape(1, -1), drop(w_pool))
    streamed = tuple(drop(v) for v in (w_in, w_out, w_up, w_down, w_ple, w_ple_gate))

    yp, cp, pp, ys, cs, ps = _layer_call(
        x_prompt.reshape(seq, D_MODEL), p_prompt.reshape(seq, PLE_DIM),
        x_sample.reshape(dec_batch * dec_seq, D_MODEL),
        p_sample.reshape(dec_batch * dec_seq, PLE_DIM), drop(state_conv), drop(state_pool),
        small, streamed, PROMPT_TILE, PROMPT_SUB_TILE, PAST_LEN)
    return (yp.reshape(1, seq, D_MODEL), ys.reshape(dec_batch, dec_seq, D_MODEL),
            cp.reshape(1, 1, CONV_HIST, CONV_WIDTH), pp.reshape(1, 1, POOL_HIST, POOL_WIDTH),
            cs.reshape(1, dec_batch, CONV_HIST, CONV_WIDTH),
            ps.reshape(1, dec_batch, POOL_HIST, POOL_WIDTH))
```

```python
import collections
import functools

import jax
import jax.numpy as jnp
from jax import lax
from jax.experimental import pallas as pl
from jax.experimental.pallas import tpu as pltpu

D_MODEL = 1024
CONV_WIDTH = 512
CONV_HIST = 2
POOL_WIDTH = 512
POOL_WINDOWS = (2, 4, 8, 16)
POOL_GROUP = 128
POOL_HIST = 15
D_FF = 4096
PLE_DIM = 256
PAST_LEN = 4096
EPS = 1e-6

U_SLOT = 16
CV_SLOT = 8
FF_CHUNK = 1024
PROMPT_TILE = 512
PROMPT_SUB_TILE = 256
STAGE_ROWS = 256
STAGE_COLS = 1024
STAGE_SLOTS = 4
VMEM_LIMIT_BYTES = 56 * 1024 * 1024

F32 = jnp.float32
BF16 = jnp.bfloat16

_STREAMED = ("w_in", "w_out", "w_up", "w_down", "w_ple", "w_gate")
_Weights = collections.namedtuple(
    "_Weights", "g_mix w_conv pool_scale g_mlp g_ple g_final w_pool " + " ".join(_STREAMED))
_Mixer = collections.namedtuple("_Mixer", "ubuf cvbuf dbuf abbuf")


def _rms(x, g):
    ms = jnp.mean(x * x, axis=-1, keepdims=True)
    return x * lax.rsqrt(ms + EPS) * g


def _dot(a, b):
    return jnp.dot(a, b, preferred_element_type=F32)


def _pack(w):
    return pltpu.bitcast(w.astype(BF16), jnp.uint32)


def _unpack(packed):
    return pltpu.bitcast(packed, BF16)


def _load_weights(hbm, resident, stage, sems):
    jobs = []
    for src, dst in zip(hbm, resident):
        rows, cols = src.shape
        assert rows % STAGE_ROWS == 0 and cols % STAGE_COLS == 0
        for r in range(0, rows, STAGE_ROWS):
            for c in range(0, cols, STAGE_COLS):
                jobs.append((src.at[pl.ds(r, STAGE_ROWS), pl.ds(c, STAGE_COLS)],
                             dst.at[pl.ds(r // 2, STAGE_ROWS // 2), pl.ds(c, STAGE_COLS)]))

    def copy(j):
        slot = j % STAGE_SLOTS
        return pltpu.make_async_copy(jobs[j][0], stage.at[slot], sems.at[slot])

    for j in range(min(STAGE_SLOTS, len(jobs))):
        copy(j).start()
    for j in range(len(jobs)):
        copy(j).wait()
        jobs[j][1][...] = _pack(stage[j % STAGE_SLOTS])
        if j + STAGE_SLOTS < len(jobs):
            copy(j + STAGE_SLOTS).start()


def _stage_project(x, w):
    h = _rms(x, w.g_mix[...]).astype(BF16)
    sec = lambda k: _dot(h, _unpack(w.w_in[:, k * CONV_WIDTH:(k + 1) * CONV_WIDTH]))
    zu, zc, zv, zb = sec(3), sec(1), sec(2), sec(0)
    return zb, zc, zv, zu


def _mix_segment(zb, zc, zv, zu, pos0, m, w):
    n = zu.shape[0]
    m.ubuf[pl.ds(U_SLOT, n), :] = zu
    s = m.ubuf[pl.ds(0, U_SLOT + n), :]
    head = min(U_SLOT, n)
    pos = pos0 + lax.broadcasted_iota(jnp.int32, (head, POOL_GROUP), 0)
    for g, win in enumerate(POOL_WINDOWS):
        s = s + pltpu.roll(s, win // 2, axis=0)
        cols = slice(g * POOL_GROUP, (g + 1) * POOL_GROUP)
        sg = s[U_SLOT:, 0:POOL_GROUP]
        u = m.ubuf[pl.ds(U_SLOT, n), cols]
        inv_head = 1.0 / jnp.minimum(pos + 1, win).astype(F32)
        m.dbuf[0:head, cols] = (sg[0:head] * inv_head - u[0:head]).astype(BF16)
        if n > head:
            m.dbuf[head:n, cols] = (sg[head:] * (1.0 / win) - u[head:]).astype(BF16)
        s = s[:, POOL_GROUP:]

    m.cvbuf[pl.ds(CV_SLOT, n), :] = zc * zv
    conv = (w.w_conv[0:1, :] * m.cvbuf[pl.ds(CV_SLOT - 2, n), :]
            + w.w_conv[1:2, :] * m.cvbuf[pl.ds(CV_SLOT - 1, n), :]
            + w.w_conv[2:3, :] * m.cvbuf[pl.ds(CV_SLOT, n), :])
    m.abbuf[:, 0:CONV_WIDTH] = (zb * conv).astype(BF16)


def _stage_out(x, m, w):
    for g in range(len(POOL_WINDOWS)):
        cols = slice(g * POOL_GROUP, (g + 1) * POOL_GROUP)
        y = _dot(m.dbuf[:, cols], _unpack(w.w_pool[g])) * w.pool_scale[:, cols]
        m.abbuf[:, CONV_WIDTH + g * POOL_GROUP:CONV_WIDTH + (g + 1) * POOL_GROUP] = y.astype(BF16)
    return x + _dot(m.abbuf[...], _unpack(w.w_out[...]))


def _stage_ffn(x, w):
    h = _rms(x, w.g_mlp[...]).astype(BF16)
    for c in range(D_FF // FF_CHUNK):
        f = jnp.maximum(_dot(h, _unpack(w.w_up[:, c * FF_CHUNK:(c + 1) * FF_CHUNK])), 0.0)
        w_down = _unpack(w.w_down[c * FF_CHUNK // 2:(c + 1) * FF_CHUNK // 2, :])
        x = x + _dot((f * f).astype(BF16), w_down)
    return x


def _stage_gate(x, p, w):
    pe = _dot(p.astype(BF16), _unpack(w.w_ple[...]))
    h = _rms(x, w.g_ple[...]).astype(BF16)
    x = x + pe * jax.nn.sigmoid(_dot(h, _unpack(w.w_gate[...])))
    return _rms(x, w.g_final[...])


def _prompt_tile(step, sub, x_ref, p_ref, y_ref, nconv_ref, npool_ref, m, w):
    n = x_ref.shape[0]
    subs = [pl.ds(k * sub, sub) for k in range(n // sub)]
    zs = [_stage_project(x_ref[r, :], w) for r in subs]
    x1 = []
    for k, r in enumerate(subs):
        mk = _Mixer(m.ubuf.at[pl.ds(k * sub, U_SLOT + sub), :],
                    m.cvbuf.at[pl.ds(k * sub, CV_SLOT + sub), :], m.dbuf.at[r, :], m.abbuf.at[r, :])
        _mix_segment(*zs[k], step * n + k * sub, mk, w)
        x1.append(_stage_out(x_ref[r, :], mk, w))
    x2 = [_stage_ffn(x, w) for x in x1]
    for k, r in enumerate(subs):
        y_ref[r, :] = _stage_gate(x2[k], p_ref[r, :], w)

    nconv_ref[...] = m.cvbuf[pl.ds(CV_SLOT + n - CONV_HIST, CONV_HIST), :]
    npool_ref[...] = m.ubuf[pl.ds(U_SLOT + n - POOL_HIST, POOL_HIST), :]
    m.ubuf[0:U_SLOT, :] = m.ubuf[pl.ds(n, U_SLOT), :]
    m.cvbuf[0:CV_SLOT, :] = m.cvbuf[pl.ds(n, CV_SLOT), :]


def _sample_rows(start, x_ref, p_ref, sconv_ref, spool_ref, y_ref, nconv_ref, npool_ref, m, w):
    nb = sconv_ref.shape[0]
    n = x_ref.shape[0] // nb
    zb, zc, zv, zu = _stage_project(x_ref[...], w)
    for b in range(nb):
        rows = slice(b * n, (b + 1) * n)
        mb = _Mixer(m.ubuf.at[b], m.cvbuf.at[b], m.dbuf.at[rows, :], m.abbuf.at[rows, :])
        mb.ubuf[pl.ds(0, U_SLOT - POOL_HIST), :] = jnp.zeros((U_SLOT - POOL_HIST, POOL_WIDTH), F32)
        mb.ubuf[pl.ds(U_SLOT - POOL_HIST, POOL_HIST), :] = spool_ref[b]
        mb.cvbuf[pl.ds(CV_SLOT - CONV_HIST, CONV_HIST), :] = sconv_ref[b]
        _mix_segment(zb[rows], zc[rows], zv[rows], zu[rows], start, mb, w)
        nconv_ref[b] = mb.cvbuf[pl.ds(CV_SLOT + n - CONV_HIST, CONV_HIST), :]
        npool_ref[b] = mb.ubuf[pl.ds(U_SLOT + n - POOL_HIST, POOL_HIST), :]
    x = _stage_out(x_ref[...], m, w)
    y_ref[...] = _stage_gate(_stage_ffn(x, w), p_ref[...], w)


def _layer_kernel(tiles, sub, start, *refs):
    refs = list(refs)
    take = lambda k: [refs.pop(0) for _ in range(k)]
    x_ref, p_ref, xs_ref, ps_ref, sconv_ref, spool_ref = take(6)
    g_mix, w_conv, pool_scale, g_mlp, g_ple, g_final, w_pool_f32 = take(7)
    hbm = take(len(_STREAMED))
    y_ref, nconv_ref, npool_ref, ys_ref, nconv_s_ref, npool_s_ref = take(6)
    resident = take(len(_STREAMED))
    w_pool, stage, sems = take(3)
    m_prompt = _Mixer(*take(4))
    m_sample = _Mixer(*take(4))
    assert not refs
    w = _Weights(g_mix, w_conv, pool_scale, g_mlp, g_ple, g_final, w_pool, *resident)
    step = pl.program_id(0)

    @pl.when(step == 0)
    def _():
        _load_weights(hbm, resident, stage, sems)
        for g in range(len(POOL_WINDOWS)):
            w_pool[g] = _pack(w_pool_f32[g])
        m_prompt.ubuf[0:U_SLOT, :] = jnp.zeros((U_SLOT, POOL_WIDTH), F32)
        m_prompt.cvbuf[0:CV_SLOT, :] = jnp.zeros((CV_SLOT, CONV_WIDTH), F32)

    @pl.when(step < tiles)
    def _():
        _prompt_tile(step, sub, x_ref, p_ref, y_ref, nconv_ref, npool_ref, m_prompt, w)

    @pl.when(step == tiles)
    def _():
        _sample_rows(start, xs_ref, ps_ref, sconv_ref, spool_ref, ys_ref, nconv_s_ref, npool_s_ref,
                     m_sample, w)


def _resident(shape):
    zeros = (0,) * len(shape)
    return pl.BlockSpec(shape, lambda i: zeros, pipeline_mode=pl.Buffered(1))


def _whole(shape):
    zeros = (0,) * len(shape)
    return pl.BlockSpec(shape, lambda i: zeros)


def _layer_call(x, p, xs, ps, sconv, spool, small, streamed, tile, sub, start):
    seq, rows = x.shape[0], xs.shape[0]
    nb = sconv.shape[0]
    n = rows // nb
    assert seq % tile == 0 and tile % sub == 0 and sub >= U_SLOT
    tiles = seq // tile
    row_spec = lambda cols: pl.BlockSpec((tile, cols), lambda i: (jnp.minimum(i, tiles - 1), 0))
    f32_out = lambda *shape: jax.ShapeDtypeStruct(shape, F32)
    mixer_scratch = lambda lead, r: [pltpu.VMEM(lead + (U_SLOT + r, POOL_WIDTH), F32),
                                     pltpu.VMEM(lead + (CV_SLOT + r, CONV_WIDTH), F32)]
    return pl.pallas_call(
        functools.partial(_layer_kernel, tiles, sub, start),
        grid=(tiles + 1,),
        in_specs=([row_spec(D_MODEL), row_spec(PLE_DIM)]
                  + [_resident(v.shape) for v in (xs, ps, sconv, spool) + tuple(small)]
                  + [pl.BlockSpec(memory_space=pl.ANY)] * len(streamed)),
        out_specs=[row_spec(D_MODEL), _whole((CONV_HIST, CONV_WIDTH)),
                   _whole((POOL_HIST, POOL_WIDTH)), _whole((rows, D_MODEL)),
                   _whole((nb, CONV_HIST, CONV_WIDTH)), _whole((nb, POOL_HIST, POOL_WIDTH))],
        out_shape=[f32_out(seq, D_MODEL), f32_out(CONV_HIST, CONV_WIDTH),
                   f32_out(POOL_HIST, POOL_WIDTH), f32_out(rows, D_MODEL),
                   f32_out(nb, CONV_HIST, CONV_WIDTH), f32_out(nb, POOL_HIST, POOL_WIDTH)],
        scratch_shapes=([pltpu.VMEM((v.shape[0] // 2, v.shape[1]), jnp.uint32) for v in streamed]
                        + [pltpu.VMEM((len(POOL_WINDOWS), POOL_GROUP // 2, POOL_GROUP),
                                      jnp.uint32),
                           pltpu.VMEM((STAGE_SLOTS, STAGE_ROWS, STAGE_COLS), F32),
                           pltpu.SemaphoreType.DMA((STAGE_SLOTS,))]
                        + mixer_scratch((), tile)
                        + [pltpu.VMEM((tile, POOL_WIDTH), BF16),
                           pltpu.VMEM((tile, 2 * CONV_WIDTH), BF16)]
                        + mixer_scratch((nb,), n)
                        + [pltpu.VMEM((rows, POOL_WIDTH), BF16),
                           pltpu.VMEM((rows, 2 * CONV_WIDTH), BF16)]),
        compiler_params=pltpu.CompilerParams(dimension_semantics=("arbitrary",),
                                             vmem_limit_bytes=VMEM_LIMIT_BYTES),
        name="encoder_layer",
    )(x, p, xs, ps, sconv, spool, *small, *streamed)


def kernel(x_prompt, x_sample, state_conv, state_pool, p_prompt, p_sample, g_mix, w_in, w_conv,
           w_pool, pool_scale, w_out, g_mlp, w_up, w_down, g_ple, w_ple, w_ple_gate, g_final):
    assert g_mix.shape[0] == 1, "single-layer trunk"
    batch, seq, _ = x_prompt.shape
    assert batch == 1
    dec_batch, dec_seq, _ = x_sample.shape

    drop = lambda v: v.reshape(v.shape[1:])
    small = (g_mix, drop(w_conv), pool_scale, g_mlp, g_ple, g_final.reshape(1, -1), drop(w_pool))
    streamed = tuple(drop(v) for v in (w_in, w_out, w_up, w_down, w_ple, w_ple_gate))

    yp, cp, pp, ys, cs, ps = _layer_call(
        x_prompt.reshape(seq, D_MODEL), p_prompt.reshape(seq, PLE_DIM),
        x_sample.reshape(dec_batch * dec_seq, D_MODEL),
        p_sample.reshape(dec_batch * dec_seq, PLE_DIM), drop(state_conv), drop(state_pool),
        small, streamed, PROMPT_TILE, PROMPT_SUB_TILE, PAST_LEN)
    return (yp.reshape(1, seq, D_MODEL), ys.reshape(dec_batch, dec_seq, D_MODEL),
            cp.reshape(1, 1, CONV_HIST, CONV_WIDTH), pp.reshape(1, 1, POOL_HIST, POOL_WIDTH),
            cs.reshape(1, dec_batch, CONV_HIST, CONV_WIDTH),
            ps.reshape(1, dec_batch, POOL_HIST, POOL_WIDTH))
```

```python
import collections
import functools

import jax
import jax.numpy as jnp
from jax import lax
from jax.experimental import pallas as pl
from jax.experimental.pallas import tpu as pltpu

D_MODEL = 1024
CONV_WIDTH = 512
CONV_HIST = 2
POOL_WIDTH = 512
POOL_WINDOWS = (2, 4, 8, 16)
POOL_GROUP = 128
POOL_HIST = 15
D_FF = 4096
PLE_DIM = 256
PAST_LEN = 4096
EPS = 1e-6

U_SLOT = 16
CV_SLOT = 8
FF_CHUNK = 1024
PROMPT_TILE = 512
PROMPT_SUB_TILE = 256
STAGE_ROWS = 256
STAGE_COLS = 1024
STAGE_SLOTS = 4
VMEM_LIMIT_BYTES = 56 * 1024 * 1024

F32 = jnp.float32
BF16 = jnp.bfloat16

_STREAMED = ("w_in", "w_out", "w_up", "w_down", "w_ple", "w_gate")
_Weights = collections.namedtuple(
    "_Weights", "g_mix w_conv pool_scale g_mlp g_ple g_final w_pool " + " ".join(_STREAMED))
_Mixer = collections.namedtuple("_Mixer", "ubuf cvbuf dbuf abbuf")


def _rms(x, g):
    ms = jnp.mean(x * x, axis=-1, keepdims=True)
    return x * lax.rsqrt(ms + EPS) * g


def _dot(a, b):
    return jnp.dot(a, b, preferred_element_type=F32)


def _pack(w):
    return pltpu.bitcast(w.astype(BF16), jnp.uint32)


def _unpack(packed):
    return pltpu.bitcast(packed, BF16)


def _load_weights(hbm, resident, stage, sems):
    jobs = []
    for src, dst in zip(hbm, resident):
        rows, cols = src.shape
        assert rows % STAGE_ROWS == 0 and cols % STAGE_COLS == 0
        for r in range(0, rows, STAGE_ROWS):
            for c in range(0, cols, STAGE_COLS):
                jobs.append((src.at[pl.ds(r, STAGE_ROWS), pl.ds(c, STAGE_COLS)],
                             dst.at[pl.ds(r // 2, STAGE_ROWS // 2), pl.ds(c, STAGE_COLS)]))

    def copy(j):
        slot = j % STAGE_SLOTS
        return pltpu.make_async_copy(jobs[j][0], stage.at[slot], sems.at[slot])

    for j in range(min(STAGE_SLOTS, len(jobs))):
        copy(j).start()
    for j in range(len(jobs)):
        copy(j).wait()
        jobs[j][1][...] = _pack(stage[j % STAGE_SLOTS])
        if j + STAGE_SLOTS < len(jobs):
            copy(j + STAGE_SLOTS).start()


def _stage_project(x, w):
    h = _rms(x, w.g_mix[...]).astype(BF16)
    sec = lambda k: _dot(h, _unpack(w.w_in[:, k * CONV_WIDTH:(k + 1) * CONV_WIDTH]))
    zu, zc, zv, zb = sec(3), sec(1), sec(2), sec(0)
    return zb, zc, zv, zu


def _mix_segment(zb, zc, zv, zu, pos0, m, w):
    n = zu.shape[0]
    m.ubuf[pl.ds(U_SLOT, n), :] = zu
    s = m.ubuf[pl.ds(0, U_SLOT + n), :]
    head = min(U_SLOT, n)
    pos = pos0 + lax.broadcasted_iota(jnp.int32, (head, POOL_GROUP), 0)
    for g, win in enumerate(POOL_WINDOWS):
        s = s + pltpu.roll(s, win // 2, axis=0)
        cols = slice(g * POOL_GROUP, (g + 1) * POOL_GROUP)
        sg = s[U_SLOT:, 0:POOL_GROUP]
        u = m.ubuf[pl.ds(U_SLOT, n), cols]
        inv_head = 1.0 / jnp.minimum(pos + 1, win).astype(F32)
        m.dbuf[0:head, cols] = (sg[0:head] * inv_head - u[0:head]).astype(BF16)
        if n > head:
            m.dbuf[head:n, cols] = (sg[head:] * (1.0 / win) - u[head:]).astype(BF16)
        s = s[:, POOL_GROUP:]

    m.cvbuf[pl.ds(CV_SLOT, n), :] = zc * zv
    conv = (w.w_conv[0] * m.cvbuf[pl.ds(CV_SLOT - 2, n), :]
            + w.w_conv[1] * m.cvbuf[pl.ds(CV_SLOT - 1, n), :]
            + w.w_conv[2] * m.cvbuf[pl.ds(CV_SLOT, n), :])
    m.abbuf[:, 0:CONV_WIDTH] = (zb * conv).astype(BF16)


def _stage_out(x, m, w):
    for g in range(len(POOL_WINDOWS)):
        cols = slice(g * POOL_GROUP, (g + 1) * POOL_GROUP)
        y = _dot(m.dbuf[:, cols], _unpack(w.w_pool[g])) * w.pool_scale[:, cols]
        m.abbuf[:, CONV_WIDTH + g * POOL_GROUP:CONV_WIDTH + (g + 1) * POOL_GROUP] = y.astype(BF16)
    return x + _dot(m.abbuf[...], _unpack(w.w_out[...]))


def _stage_ffn(x, w):
    h = _rms(x, w.g_mlp[...]).astype(BF16)
    for c in range(D_FF // FF_CHUNK):
        f = jnp.maximum(_dot(h, _unpack(w.w_up[:, c * FF_CHUNK:(c + 1) * FF_CHUNK])), 0.0)
        w_down = _unpack(w.w_down[c * FF_CHUNK // 2:(c + 1) * FF_CHUNK // 2, :])
        x = x + _dot((f * f).astype(BF16), w_down)
    return x


def _stage_gate(x, p, w):
    pe = _dot(p.astype(BF16), _unpack(w.w_ple[...]))
    h = _rms(x, w.g_ple[...]).astype(BF16)
    x = x + pe * jax.nn.sigmoid(_dot(h, _unpack(w.w_gate[...])))
    return _rms(x, w.g_final[...])


def _prompt_tile(step, sub, x_ref, p_ref, y_ref, nconv_ref, npool_ref, m, w):
    n = x_ref.shape[0]
    subs = [pl.ds(k * sub, sub) for k in range(n // sub)]
    zs = [_stage_project(x_ref[r, :], w) for r in subs]
    x1 = []
    for k, r in enumerate(subs):
        mk = _Mixer(m.ubuf.at[pl.ds(k * sub, U_SLOT + sub), :],
                    m.cvbuf.at[pl.ds(k * sub, CV_SLOT + sub), :], m.dbuf.at[r, :], m.abbuf.at[r, :])
        _mix_segment(*zs[k], step * n + k * sub, mk, w)
        x1.append(_stage_out(x_ref[r, :], mk, w))
    x2 = [_stage_ffn(x, w) for x in x1]
    for k, r in enumerate(subs):
        y_ref[r, :] = _stage_gate(x2[k], p_ref[r, :], w)

    nconv_ref[...] = m.cvbuf[pl.ds(CV_SLOT + n - CONV_HIST, CONV_HIST), :]
    for t in range(POOL_HIST):
        npool_ref[t] = m.ubuf[pl.ds(U_SLOT + n - POOL_HIST + t, 1), :]
    m.ubuf[0:U_SLOT, :] = m.ubuf[pl.ds(n, U_SLOT), :]
    m.cvbuf[0:CV_SLOT, :] = m.cvbuf[pl.ds(n, CV_SLOT), :]


def _sample_rows(start, x_ref, p_ref, sconv_ref, spool_ref, y_ref, nconv_ref, npool_ref, m, w):
    nb = sconv_ref.shape[0]
    n = x_ref.shape[0] // nb
    zb, zc, zv, zu = _stage_project(x_ref[...], w)
    for t in range(POOL_HIST):
        frame = spool_ref[t]
        for b in range(nb):
            m.ubuf[b, pl.ds(U_SLOT - POOL_HIST + t, 1), :] = frame[b:b + 1, :]
    for b in range(nb):
        rows = slice(b * n, (b + 1) * n)
        mb = _Mixer(m.ubuf.at[b], m.cvbuf.at[b], m.dbuf.at[rows, :], m.abbuf.at[rows, :])
        mb.ubuf[pl.ds(0, U_SLOT - POOL_HIST), :] = jnp.zeros((U_SLOT - POOL_HIST, POOL_WIDTH), F32)
        mb.cvbuf[pl.ds(CV_SLOT - CONV_HIST, CONV_HIST), :] = sconv_ref[b]
        _mix_segment(zb[rows], zc[rows], zv[rows], zu[rows], start, mb, w)
        nconv_ref[b] = mb.cvbuf[pl.ds(CV_SLOT + n - CONV_HIST, CONV_HIST), :]
    for t in range(POOL_HIST):
        for b in range(nb):
            npool_ref[t, pl.ds(b, 1), :] = m.ubuf[b, pl.ds(U_SLOT + n - POOL_HIST + t, 1), :]
    x = _stage_out(x_ref[...], m, w)
    y_ref[...] = _stage_gate(_stage_ffn(x, w), p_ref[...], w)


def _layer_kernel(tiles, sub, start, *refs):
    refs = list(refs)
    take = lambda k: [refs.pop(0) for _ in range(k)]
    x_ref, p_ref, xs_ref, ps_ref, sconv_ref, spool_ref = take(6)
    g_mix, w_conv, pool_scale, g_mlp, g_ple, g_final, w_pool_f32 = take(7)
    hbm = take(len(_STREAMED))
    y_ref, nconv_ref, npool_ref, ys_ref, nconv_s_ref, npool_s_ref = take(6)
    resident = take(len(_STREAMED))
    w_pool, stage, sems = take(3)
    m_prompt = _Mixer(*take(4))
    m_sample = _Mixer(*take(4))
    assert not refs
    w = _Weights(g_mix, w_conv, pool_scale, g_mlp, g_ple, g_final, w_pool, *resident)
    step = pl.program_id(0)

    @pl.when(step == 0)
    def _():
        _load_weights(hbm, resident, stage, sems)
        for g in range(len(POOL_WINDOWS)):
            w_pool[g] = _pack(w_pool_f32[g])
        m_prompt.ubuf[0:U_SLOT, :] = jnp.zeros((U_SLOT, POOL_WIDTH), F32)
        m_prompt.cvbuf[0:CV_SLOT, :] = jnp.zeros((CV_SLOT, CONV_WIDTH), F32)

    @pl.when(step < tiles)
    def _():
        _prompt_tile(step, sub, x_ref, p_ref, y_ref, nconv_ref, npool_ref, m_prompt, w)

    @pl.when(step == tiles)
    def _():
        _sample_rows(start, xs_ref, ps_ref, sconv_ref, spool_ref, ys_ref, nconv_s_ref, npool_s_ref,
                     m_sample, w)


def _resident(shape):
    zeros = (0,) * len(shape)
    return pl.BlockSpec(shape, lambda i: zeros, pipeline_mode=pl.Buffered(1))


def _whole(shape):
    zeros = (0,) * len(shape)
    return pl.BlockSpec(shape, lambda i: zeros)


def _layer_call(x, p, xs, ps, sconv, spool, small, streamed, tile, sub, start):
    seq, rows = x.shape[0], xs.shape[0]
    nb = sconv.shape[0]
    n = rows // nb
    assert seq % tile == 0 and tile % sub == 0 and sub >= U_SLOT
    tiles = seq // tile
    row_spec = lambda cols: pl.BlockSpec((tile, cols), lambda i: (jnp.minimum(i, tiles - 1), 0))
    f32_out = lambda *shape: jax.ShapeDtypeStruct(shape, F32)
    mixer_scratch = lambda lead, r: [pltpu.VMEM(lead + (U_SLOT + r, POOL_WIDTH), F32),
                                     pltpu.VMEM(lead + (CV_SLOT + r, CONV_WIDTH), F32)]
    return pl.pallas_call(
        functools.partial(_layer_kernel, tiles, sub, start),
        grid=(tiles + 1,),
        in_specs=([row_spec(D_MODEL), row_spec(PLE_DIM)]
                  + [_resident(v.shape) for v in (xs, ps, sconv, spool) + tuple(small)]
                  + [pl.BlockSpec(memory_space=pl.ANY)] * len(streamed)),
        out_specs=[row_spec(D_MODEL), _whole((CONV_HIST, CONV_WIDTH)),
                   _whole((POOL_HIST, 1, POOL_WIDTH)), _whole((rows, D_MODEL)),
                   _whole((nb, CONV_HIST, CONV_WIDTH)), _whole((POOL_HIST, nb, POOL_WIDTH))],
        out_shape=[f32_out(seq, D_MODEL), f32_out(CONV_HIST, CONV_WIDTH),
                   f32_out(POOL_HIST, 1, POOL_WIDTH), f32_out(rows, D_MODEL),
                   f32_out(nb, CONV_HIST, CONV_WIDTH), f32_out(POOL_HIST, nb, POOL_WIDTH)],
        scratch_shapes=([pltpu.VMEM((v.shape[0] // 2, v.shape[1]), jnp.uint32) for v in streamed]
                        + [pltpu.VMEM((len(POOL_WINDOWS), POOL_GROUP // 2, POOL_GROUP),
                                      jnp.uint32),
                           pltpu.VMEM((STAGE_SLOTS, STAGE_ROWS, STAGE_COLS), F32),
                           pltpu.SemaphoreType.DMA((STAGE_SLOTS,))]
                        + mixer_scratch((), tile)
                        + [pltpu.VMEM((tile, POOL_WIDTH), BF16),
                           pltpu.VMEM((tile, 2 * CONV_WIDTH), BF16)]
                        + mixer_scratch((nb,), n)
                        + [pltpu.VMEM((rows, POOL_WIDTH), BF16),
                           pltpu.VMEM((rows, 2 * CONV_WIDTH), BF16)]),
        compiler_params=pltpu.CompilerParams(dimension_semantics=("arbitrary",),
                                             vmem_limit_bytes=VMEM_LIMIT_BYTES),
        name="encoder_layer",
    )(x, p, xs, ps, sconv, spool, *small, *streamed)


def kernel(x_prompt, x_sample, state_conv, state_pool, p_prompt, p_sample, g_mix, w_in, w_conv,
           w_pool, pool_scale, w_out, g_mlp, w_up, w_down, g_ple, w_ple, w_ple_gate, g_final):
    assert g_mix.shape[0] == 1, "single-layer trunk"
    batch, seq, _ = x_prompt.shape
    assert batch == 1
    dec_batch, dec_seq, _ = x_sample.shape

    drop = lambda v: v.reshape(v.shape[1:])
    time_major = lambda v: jnp.swapaxes(v, -3, -2)
    small = (g_mix, time_major(w_conv), pool_scale, g_mlp, g_ple, g_final.reshape(1, -1),
             drop(w_pool))
    streamed = tuple(drop(v) for v in (w_in, w_out, w_up, w_down, w_ple, w_ple_gate))

    yp, cp, pp, ys, cs, ps = _layer_call(
        x_prompt.reshape(seq, D_MODEL), p_prompt.reshape(seq, PLE_DIM),
        x_sample.reshape(dec_batch * dec_seq, D_MODEL),
        p_sample.reshape(dec_batch * dec_seq, PLE_DIM), drop(state_conv),
        drop(time_major(state_pool)), small, streamed, PROMPT_TILE, PROMPT_SUB_TILE, PAST_LEN)
    return (yp.reshape(1, seq, D_MODEL), ys.reshape(dec_batch, dec_seq, D_MODEL),
            cp.reshape(1, 1, CONV_HIST, CONV_WIDTH),
            time_major(pp.reshape(1, POOL_HIST, 1, POOL_WIDTH)),
            cs.reshape(1, dec_batch, CONV_HIST, CONV_WIDTH),
            time_major(ps.reshape(1, POOL_HIST, dec_batch, POOL_WIDTH)))
```

```python
import collections
import functools

import jax
import jax.numpy as jnp
from jax import lax
from jax.experimental import pallas as pl
from jax.experimental.pallas import tpu as pltpu

D_MODEL = 1024
CONV_WIDTH = 512
CONV_HIST = 2
POOL_WIDTH = 512
POOL_WINDOWS = (2, 4, 8, 16)
POOL_GROUP = 128
POOL_HIST = 15
D_FF = 4096
PLE_DIM = 256
PAST_LEN = 4096
EPS = 1e-6

U_SLOT = 16
CV_SLOT = 8
FF_CHUNK = 1024
PROMPT_TILE = 512
PROMPT_SUB_TILE = 256
STAGE_ROWS = 256
STAGE_COLS = 1024
STAGE_SLOTS = 8
VMEM_LIMIT_BYTES = 56 * 1024 * 1024

F32 = jnp.float32
BF16 = jnp.bfloat16

_STREAMED = ("w_in", "w_out", "w_up", "w_down", "w_ple", "w_gate")
_Weights = collections.namedtuple(
    "_Weights", "g_mix w_conv pool_scale g_mlp g_ple g_final w_pool " + " ".join(_STREAMED))
_Mixer = collections.namedtuple("_Mixer", "ubuf cvbuf dbuf abbuf")


def _rms(x, g):
    ms = jnp.mean(x * x, axis=-1, keepdims=True)
    return x * lax.rsqrt(ms + EPS) * g


def _dot(a, b):
    return jnp.dot(a, b, preferred_element_type=F32)


def _pack(w):
    return pltpu.bitcast(w.astype(BF16), jnp.uint32)


def _unpack(packed):
    return pltpu.bitcast(packed, BF16)


def _load_weights(hbm, resident, stage, sems):
    jobs = []
    for src, dst in zip(hbm, resident):
        rows, cols = src.shape
        assert rows % STAGE_ROWS == 0 and cols % STAGE_COLS == 0
        for r in range(0, rows, STAGE_ROWS):
            for c in range(0, cols, STAGE_COLS):
                jobs.append((src.at[pl.ds(r, STAGE_ROWS), pl.ds(c, STAGE_COLS)],
                             dst.at[pl.ds(r // 2, STAGE_ROWS // 2), pl.ds(c, STAGE_COLS)]))

    def copy(j):
        slot = j % STAGE_SLOTS
        return pltpu.make_async_copy(jobs[j][0], stage.at[slot], sems.at[slot])

    for j in range(min(STAGE_SLOTS, len(jobs))):
        copy(j).start()
    for j in range(len(jobs)):
        copy(j).wait()
        jobs[j][1][...] = _pack(stage[j % STAGE_SLOTS])
        if j + STAGE_SLOTS < len(jobs):
            copy(j + STAGE_SLOTS).start()


def _stage_project(x, w):
    h = _rms(x, w.g_mix[...]).astype(BF16)
    sec = lambda k: _dot(h, _unpack(w.w_in[:, k * CONV_WIDTH:(k + 1) * CONV_WIDTH]))
    zu, zc, zv, zb = sec(3), sec(1), sec(2), sec(0)
    return zb, zc, zv, zu


def _mix_segment(zb, zc, zv, zu, pos0, m, w):
    n = zu.shape[0]
    m.ubuf[pl.ds(U_SLOT, n), :] = zu
    s = m.ubuf[pl.ds(0, U_SLOT + n), :]
    head = min(U_SLOT, n)
    pos = pos0 + lax.broadcasted_iota(jnp.int32, (head, POOL_GROUP), 0)
    for g, win in enumerate(POOL_WINDOWS):
        s = s + pltpu.roll(s, win // 2, axis=0)
        cols = slice(g * POOL_GROUP, (g + 1) * POOL_GROUP)
        sg = s[U_SLOT:, 0:POOL_GROUP]
        u = m.ubuf[pl.ds(U_SLOT, n), cols]
        inv_head = 1.0 / jnp.minimum(pos + 1, win).astype(F32)
        m.dbuf[0:head, cols] = (sg[0:head] * inv_head - u[0:head]).astype(BF16)
        if n > head:
            m.dbuf[head:n, cols] = (sg[head:] * (1.0 / win) - u[head:]).astype(BF16)
        s = s[:, POOL_GROUP:]

    m.cvbuf[pl.ds(CV_SLOT, n), :] = zc * zv
    conv = (w.w_conv[0] * m.cvbuf[pl.ds(CV_SLOT - 2, n), :]
            + w.w_conv[1] * m.cvbuf[pl.ds(CV_SLOT - 1, n), :]
            + w.w_conv[2] * m.cvbuf[pl.ds(CV_SLOT, n), :])
    m.abbuf[:, 0:CONV_WIDTH] = (zb * conv).astype(BF16)


def _stage_out(x, m, w):
    for g in range(len(POOL_WINDOWS)):
        cols = slice(g * POOL_GROUP, (g + 1) * POOL_GROUP)
        y = _dot(m.dbuf[:, cols], _unpack(w.w_pool[g])) * w.pool_scale[:, cols]
        m.abbuf[:, CONV_WIDTH + g * POOL_GROUP:CONV_WIDTH + (g + 1) * POOL_GROUP] = y.astype(BF16)
    return x + _dot(m.abbuf[...], _unpack(w.w_out[...]))


def _stage_ffn(x, w):
    h = _rms(x, w.g_mlp[...]).astype(BF16)
    for c in range(D_FF // FF_CHUNK):
        f = jnp.maximum(_dot(h, _unpack(w.w_up[:, c * FF_CHUNK:(c + 1) * FF_CHUNK])), 0.0)
        w_down = _unpack(w.w_down[c * FF_CHUNK // 2:(c + 1) * FF_CHUNK // 2, :])
        x = x + _dot((f * f).astype(BF16), w_down)
    return x


def _stage_gate(x, p, w):
    pe = _dot(p.astype(BF16), _unpack(w.w_ple[...]))
    h = _rms(x, w.g_ple[...]).astype(BF16)
    x = x + pe * jax.nn.sigmoid(_dot(h, _unpack(w.w_gate[...])))
    return _rms(x, w.g_final[...])


def _prompt_tile(step, sub, x_ref, p_ref, y_ref, nconv_ref, npool_ref, m, w):
    n = x_ref.shape[0]
    subs = [pl.ds(k * sub, sub) for k in range(n // sub)]
    zs = [_stage_project(x_ref[r, :], w) for r in subs]
    x1 = []
    for k, r in enumerate(subs):
        mk = _Mixer(m.ubuf.at[pl.ds(k * sub, U_SLOT + sub), :],
                    m.cvbuf.at[pl.ds(k * sub, CV_SLOT + sub), :], m.dbuf.at[r, :], m.abbuf.at[r, :])
        _mix_segment(*zs[k], step * n + k * sub, mk, w)
        x1.append(_stage_out(x_ref[r, :], mk, w))
    x2 = [_stage_ffn(x, w) for x in x1]
    for k, r in enumerate(subs):
        y_ref[r, :] = _stage_gate(x2[k], p_ref[r, :], w)

    nconv_ref[...] = m.cvbuf[pl.ds(CV_SLOT + n - CONV_HIST, CONV_HIST), :]
    for t in range(POOL_HIST):
        npool_ref[t] = m.ubuf[pl.ds(U_SLOT + n - POOL_HIST + t, 1), :]
    m.ubuf[0:U_SLOT, :] = m.ubuf[pl.ds(n, U_SLOT), :]
    m.cvbuf[0:CV_SLOT, :] = m.cvbuf[pl.ds(n, CV_SLOT), :]


def _sample_rows(start, x_ref, p_ref, sconv_ref, spool_ref, y_ref, nconv_ref, npool_ref, m, w):
    nb = sconv_ref.shape[0]
    n = x_ref.shape[0] // nb
    zb, zc, zv, zu = _stage_project(x_ref[...], w)
    for t in range(POOL_HIST):
        frame = spool_ref[t]
        for b in range(nb):
            m.ubuf[b, pl.ds(U_SLOT - POOL_HIST + t, 1), :] = frame[b:b + 1, :]
    for b in range(nb):
        rows = slice(b * n, (b + 1) * n)
        mb = _Mixer(m.ubuf.at[b], m.cvbuf.at[b], m.dbuf.at[rows, :], m.abbuf.at[rows, :])
        mb.ubuf[pl.ds(0, U_SLOT - POOL_HIST), :] = jnp.zeros((U_SLOT - POOL_HIST, POOL_WIDTH), F32)
        mb.cvbuf[pl.ds(CV_SLOT - CONV_HIST, CONV_HIST), :] = sconv_ref[b]
        _mix_segment(zb[rows], zc[rows], zv[rows], zu[rows], start, mb, w)
        nconv_ref[b] = mb.cvbuf[pl.ds(CV_SLOT + n - CONV_HIST, CONV_HIST), :]
    for t in range(POOL_HIST):
        for b in range(nb):
            npool_ref[t, pl.ds(b, 1), :] = m.ubuf[b, pl.ds(U_SLOT + n - POOL_HIST + t, 1), :]
    x = _stage_out(x_ref[...], m, w)
    y_ref[...] = _stage_gate(_stage_ffn(x, w), p_ref[...], w)


def _layer_kernel(tiles, sub, start, *refs):
    refs = list(refs)
    take = lambda k: [refs.pop(0) for _ in range(k)]
    x_ref, p_ref, xs_ref, ps_ref, sconv_ref, spool_ref = take(6)
    g_mix, w_conv, pool_scale, g_mlp, g_ple, g_final, w_pool_f32 = take(7)
    hbm = take(len(_STREAMED))
    y_ref, nconv_ref, npool_ref, ys_ref, nconv_s_ref, npool_s_ref = take(6)
    resident = take(len(_STREAMED))
    w_pool, stage, sems = take(3)
    m_prompt = _Mixer(*take(4))
    m_sample = _Mixer(*take(4))
    assert not refs
    w = _Weights(g_mix, w_conv, pool_scale, g_mlp, g_ple, g_final, w_pool, *resident)
    step = pl.program_id(0)

    @pl.when(step == 0)
    def _():
        _load_weights(hbm, resident, stage, sems)
        for g in range(len(POOL_WINDOWS)):
            w_pool[g] = _pack(w_pool_f32[g])
        m_prompt.ubuf[0:U_SLOT, :] = jnp.zeros((U_SLOT, POOL_WIDTH), F32)
        m_prompt.cvbuf[0:CV_SLOT, :] = jnp.zeros((CV_SLOT, CONV_WIDTH), F32)

    @pl.when(step < tiles)
    def _():
        _prompt_tile(step, sub, x_ref, p_ref, y_ref, nconv_ref, npool_ref, m_prompt, w)

    @pl.when(step == tiles)
    def _():
        _sample_rows(start, xs_ref, ps_ref, sconv_ref, spool_ref, ys_ref, nconv_s_ref, npool_s_ref,
                     m_sample, w)


def _resident(shape):
    zeros = (0,) * len(shape)
    return pl.BlockSpec(shape, lambda i: zeros, pipeline_mode=pl.Buffered(1))


def _whole(shape):
    zeros = (0,) * len(shape)
    return pl.BlockSpec(shape, lambda i: zeros)


def _layer_call(x, p, xs, ps, sconv, spool, small, streamed, tile, sub, start):
    seq, rows = x.shape[0], xs.shape[0]
    nb = sconv.shape[0]
    n = rows // nb
    assert seq % tile == 0 and tile % sub == 0 and sub >= U_SLOT
    tiles = seq // tile
    row_spec = lambda cols: pl.BlockSpec((tile, cols), lambda i: (jnp.minimum(i, tiles - 1), 0))
    f32_out = lambda *shape: jax.ShapeDtypeStruct(shape, F32)
    mixer_scratch = lambda lead, r: [pltpu.VMEM(lead + (U_SLOT + r, POOL_WIDTH), F32),
                                     pltpu.VMEM(lead + (CV_SLOT + r, CONV_WIDTH), F32)]
    return pl.pallas_call(
        functools.partial(_layer_kernel, tiles, sub, start),
        grid=(tiles + 1,),
        in_specs=([row_spec(D_MODEL), row_spec(PLE_DIM)]
                  + [_resident(v.shape) for v in (xs, ps, sconv, spool) + tuple(small)]
                  + [pl.BlockSpec(memory_space=pl.ANY)] * len(streamed)),
        out_specs=[row_spec(D_MODEL), _whole((CONV_HIST, CONV_WIDTH)),
                   _whole((POOL_HIST, 1, POOL_WIDTH)), _whole((rows, D_MODEL)),
                   _whole((nb, CONV_HIST, CONV_WIDTH)), _whole((POOL_HIST, nb, POOL_WIDTH))],
        out_shape=[f32_out(seq, D_MODEL), f32_out(CONV_HIST, CONV_WIDTH),
                   f32_out(POOL_HIST, 1, POOL_WIDTH), f32_out(rows, D_MODEL),
                   f32_out(nb, CONV_HIST, CONV_WIDTH), f32_out(POOL_HIST, nb, POOL_WIDTH)],
        scratch_shapes=([pltpu.VMEM((v.shape[0] // 2, v.shape[1]), jnp.uint32) for v in streamed]
                        + [pltpu.VMEM((len(POOL_WINDOWS), POOL_GROUP // 2, POOL_GROUP),
                                      jnp.uint32),
                           pltpu.VMEM((STAGE_SLOTS, STAGE_ROWS, STAGE_COLS), F32),
                           pltpu.SemaphoreType.DMA((STAGE_SLOTS,))]
                        + mixer_scratch((), tile)
                        + [pltpu.VMEM((tile, POOL_WIDTH), BF16),
                           pltpu.VMEM((tile, 2 * CONV_WIDTH), BF16)]
                        + mixer_scratch((nb,), n)
                        + [pltpu.VMEM((rows, POOL_WIDTH), BF16),
                           pltpu.VMEM((rows, 2 * CONV_WIDTH), BF16)]),
        compiler_params=pltpu.CompilerParams(dimension_semantics=("arbitrary",),
                                             vmem_limit_bytes=VMEM_LIMIT_BYTES),
        name="encoder_layer",
    )(x, p, xs, ps, sconv, spool, *small, *streamed)


def kernel(x_prompt, x_sample, state_conv, state_pool, p_prompt, p_sample, g_mix, w_in, w_conv,
           w_pool, pool_scale, w_out, g_mlp, w_up, w_down, g_ple, w_ple, w_ple_gate, g_final):
    assert g_mix.shape[0] == 1, "single-layer trunk"
    batch, seq, _ = x_prompt.shape
    assert batch == 1
    dec_batch, dec_seq, _ = x_sample.shape

    drop = lambda v: v.reshape(v.shape[1:])
    time_major = lambda v: jnp.swapaxes(v, -3, -2)
    small = (g_mix, time_major(w_conv), pool_scale, g_mlp, g_ple, g_final.reshape(1, -1),
             drop(w_pool))
    streamed = tuple(drop(v) for v in (w_in, w_out, w_up, w_down, w_ple, w_ple_gate))

    yp, cp, pp, ys, cs, ps = _layer_call(
        x_prompt.reshape(seq, D_MODEL), p_prompt.reshape(seq, PLE_DIM),
        x_sample.reshape(dec_batch * dec_seq, D_MODEL),
        p_sample.reshape(dec_batch * dec_seq, PLE_DIM), drop(state_conv),
        drop(time_major(state_pool)), small, streamed, PROMPT_TILE, PROMPT_SUB_TILE, PAST_LEN)
    return (yp.reshape(1, seq, D_MODEL), ys.reshape(dec_batch, dec_seq, D_MODEL),
            cp.reshape(1, 1, CONV_HIST, CONV_WIDTH),
            time_major(pp.reshape(1, POOL_HIST, 1, POOL_WIDTH)),
            cs.reshape(1, dec_batch, CONV_HIST, CONV_WIDTH),
            time_major(ps.reshape(1, POOL_HIST, dec_batch, POOL_WIDTH)))
```

```python
import collections
import functools

import jax
import jax.numpy as jnp
from jax import lax
from jax.experimental import pallas as pl
from jax.experimental.pallas import tpu as pltpu

D_MODEL = 1024
CONV_WIDTH = 512
CONV_HIST = 2
POOL_WIDTH = 512
POOL_WINDOWS = (2, 4, 8, 16)
POOL_GROUP = 128
POOL_HIST = 15
D_FF = 4096
PLE_DIM = 256
PAST_LEN = 4096
EPS = 1e-6

U_SLOT = 16
CV_SLOT = 8
FF_CHUNK = 1024
PROMPT_TILE = 512
PROMPT_SUB_TILE = 256
STAGE_ROWS = 256
STAGE_COLS = 1024
STAGE_SLOTS = 8
VMEM_LIMIT_BYTES = 60 * 1024 * 1024

F32 = jnp.float32
BF16 = jnp.bfloat16

_STREAMED = ("w_in", "w_out", "w_up", "w_down", "w_ple", "w_gate")
_Weights = collections.namedtuple(
    "_Weights", "g_mix w_conv pool_scale g_mlp g_ple g_final w_pool " + " ".join(_STREAMED))
_Mixer = collections.namedtuple("_Mixer", "ubuf cvbuf")


def _rms(x, g):
    ms = jnp.mean(x * x, axis=-1, keepdims=True)
    return x * lax.rsqrt(ms + EPS) * g


def _dot(a, b):
    return jnp.dot(a, b, preferred_element_type=F32)


def _pack(w):
    return pltpu.bitcast(w.astype(BF16), jnp.uint32)


def _unpack(packed):
    return pltpu.bitcast(packed, BF16)


class _WeightStream:
    def __init__(self, hbm, resident, stage, sems):
        w = dict(zip(_STREAMED, zip(hbm, resident)))
        self.stage, self.sems, self.jobs, self.done = stage, sems, [], 0

        def queue(group, name, rows, cols):
            src, dst = w[name]
            for r in range(rows.start, rows.stop, STAGE_ROWS):
                for c in range(cols.start, cols.stop, STAGE_COLS):
                    self.jobs.append((group,
                                      src.at[pl.ds(r, STAGE_ROWS), pl.ds(c, STAGE_COLS)],
                                      dst.at[pl.ds(r // 2, STAGE_ROWS // 2), pl.ds(c, STAGE_COLS)]))

        whole = lambda name: (name, slice(0, w[name][0].shape[0]), slice(0, w[name][0].shape[1]))
        queue("w_in", *whole("w_in"))
        queue("w_out", *whole("w_out"))
        for c in range(D_FF // FF_CHUNK):
            chunk = slice(c * FF_CHUNK, (c + 1) * FF_CHUNK)
            queue(("ffn", c), "w_up", slice(0, D_MODEL), chunk)
            queue(("ffn", c), "w_down", chunk, slice(0, D_MODEL))
        queue("gate", *whole("w_ple"))
        queue("gate", *whole("w_gate"))
        for j in range(min(STAGE_SLOTS, len(self.jobs))):
            self._copy(j).start()

    def _copy(self, j):
        slot = j % STAGE_SLOTS
        return pltpu.make_async_copy(self.jobs[j][1], self.stage.at[slot], self.sems.at[slot])

    def need(self, group):
        last = max(j for j, job in enumerate(self.jobs) if job[0] == group)
        while self.done <= last:
            j = self.done
            self._copy(j).wait()
            self.jobs[j][2][...] = _pack(self.stage[j % STAGE_SLOTS])
            if j + STAGE_SLOTS < len(self.jobs):
                self._copy(j + STAGE_SLOTS).start()
            self.done += 1

    def finish(self):
        self.need(self.jobs[-1][0])


def _no_wait(group):
    pass


def _stage_project(x, w, need=_no_wait):
    h = _rms(x, w.g_mix[...]).astype(BF16)
    need("w_in")
    sec = lambda k: _dot(h, _unpack(w.w_in[:, k * CONV_WIDTH:(k + 1) * CONV_WIDTH]))
    zu, zc, zv, zb = sec(3), sec(1), sec(2), sec(0)
    return zb, zc, zv, zu


def _mix_segment(zb, zc, zv, zu, pos0, m, w):
    n = zu.shape[0]
    m.ubuf[pl.ds(U_SLOT, n), :] = zu
    s = m.ubuf[pl.ds(0, U_SLOT + n), :]
    head = min(U_SLOT, n)
    pos = pos0 + lax.broadcasted_iota(jnp.int32, (head, POOL_GROUP), 0)
    diffs = []
    for g, win in enumerate(POOL_WINDOWS):
        s = s + pltpu.roll(s, win // 2, axis=0)
        cols = slice(g * POOL_GROUP, (g + 1) * POOL_GROUP)
        sg = s[U_SLOT:, 0:POOL_GROUP]
        u = m.ubuf[pl.ds(U_SLOT, n), cols]
        inv_head = 1.0 / jnp.minimum(pos + 1, win).astype(F32)
        d = [(sg[0:head] * inv_head - u[0:head]).astype(BF16)]
        if n > head:
            d.append((sg[head:] * (1.0 / win) - u[head:]).astype(BF16))
        diffs.append(jnp.concatenate(d, axis=0))
        s = s[:, POOL_GROUP:]

    m.cvbuf[pl.ds(CV_SLOT, n), :] = zc * zv
    conv = (w.w_conv[0] * m.cvbuf[pl.ds(CV_SLOT - 2, n), :]
            + w.w_conv[1] * m.cvbuf[pl.ds(CV_SLOT - 1, n), :]
            + w.w_conv[2] * m.cvbuf[pl.ds(CV_SLOT, n), :])
    return (zb * conv).astype(BF16), diffs


def _stage_out(x, a, diffs, w, need=_no_wait):
    need("w_out")
    ab = [a]
    for g, d in enumerate(diffs):
        cols = slice(g * POOL_GROUP, (g + 1) * POOL_GROUP)
        ab.append((_dot(d, _unpack(w.w_pool[g])) * w.pool_scale[:, cols]).astype(BF16))
    return x + _dot(jnp.concatenate(ab, axis=1), _unpack(w.w_out[...]))


def _stage_ffn(xs, w, need=None):
    xs = list(xs)
    norm = lambda x: _rms(x, w.g_mlp[...]).astype(BF16)
    chunks = range(D_FF // FF_CHUNK)

    def apply(k, h, c):
        f = jnp.maximum(_dot(h, _unpack(w.w_up[:, c * FF_CHUNK:(c + 1) * FF_CHUNK])), 0.0)
        w_down = _unpack(w.w_down[c * FF_CHUNK // 2:(c + 1) * FF_CHUNK // 2, :])
        xs[k] = xs[k] + _dot((f * f).astype(BF16), w_down)

    if need is None:
        for k in range(len(xs)):
            h = norm(xs[k])
            for c in chunks:
                apply(k, h, c)
    else:
        hs = [norm(x) for x in xs]
        for c in chunks:
            need(("ffn", c))
            for k in range(len(xs)):
                apply(k, hs[k], c)
    return xs


def _stage_gate(x, p, w, need=_no_wait):
    need("gate")
    pe = _dot(p.astype(BF16), _unpack(w.w_ple[...]))
    h = _rms(x, w.g_ple[...]).astype(BF16)
    x = x + pe * jax.nn.sigmoid(_dot(h, _unpack(w.w_gate[...])))
    return _rms(x, w.g_final[...])


def _prompt_tile(step, sub, x_ref, p_ref, y_ref, nconv_ref, npool_ref, m, w, stream=None):
    need = _no_wait if stream is None else stream.need
    n = x_ref.shape[0]
    subs = [pl.ds(k * sub, sub) for k in range(n // sub)]
    zs = [_stage_project(x_ref[r, :], w, need) for r in subs]
    x1 = []
    for k, r in enumerate(subs):
        mk = _Mixer(m.ubuf.at[pl.ds(k * sub, U_SLOT + sub), :],
                    m.cvbuf.at[pl.ds(k * sub, CV_SLOT + sub), :])
        a, diffs = _mix_segment(*zs[k], step * n + k * sub, mk, w)
        x1.append(_stage_out(x_ref[r, :], a, diffs, w, need))
    x2 = _stage_ffn(x1, w, None if stream is None else need)
    for k, r in enumerate(subs):
        y_ref[r, :] = _stage_gate(x2[k], p_ref[r, :], w, need)

    nconv_ref[...] = m.cvbuf[pl.ds(CV_SLOT + n - CONV_HIST, CONV_HIST), :]
    for t in range(POOL_HIST):
        npool_ref[t] = m.ubuf[pl.ds(U_SLOT + n - POOL_HIST + t, 1), :]
    m.ubuf[0:U_SLOT, :] = m.ubuf[pl.ds(n, U_SLOT), :]
    m.cvbuf[0:CV_SLOT, :] = m.cvbuf[pl.ds(n, CV_SLOT), :]


def _sample_rows(start, x_ref, p_ref, sconv_ref, spool_ref, y_ref, nconv_ref, npool_ref, m, w):
    nb = sconv_ref.shape[0]
    n = x_ref.shape[0] // nb
    zb, zc, zv, zu = _stage_project(x_ref[...], w)
    for t in range(POOL_HIST):
        frame = spool_ref[t]
        for b in range(nb):
            m.ubuf[b, pl.ds(U_SLOT - POOL_HIST + t, 1), :] = frame[b:b + 1, :]
    mixed = []
    for b in range(nb):
        rows = slice(b * n, (b + 1) * n)
        mb = _Mixer(m.ubuf.at[b], m.cvbuf.at[b])
        mb.ubuf[pl.ds(0, U_SLOT - POOL_HIST), :] = jnp.zeros((U_SLOT - POOL_HIST, POOL_WIDTH), F32)
        mb.cvbuf[pl.ds(CV_SLOT - CONV_HIST, CONV_HIST), :] = sconv_ref[b]
        mixed.append(_mix_segment(zb[rows], zc[rows], zv[rows], zu[rows], start, mb, w))
        nconv_ref[b] = mb.cvbuf[pl.ds(CV_SLOT + n - CONV_HIST, CONV_HIST), :]
    a = jnp.concatenate([a_b for a_b, _ in mixed], axis=0)
    diffs = [jnp.concatenate([d_b[g] for _, d_b in mixed], axis=0)
             for g in range(len(POOL_WINDOWS))]
    for t in range(POOL_HIST):
        for b in range(nb):
            npool_ref[t, pl.ds(b, 1), :] = m.ubuf[b, pl.ds(U_SLOT + n - POOL_HIST + t, 1), :]
    x = _stage_out(x_ref[...], a, diffs, w)
    y_ref[...] = _stage_gate(_stage_ffn([x], w)[0], p_ref[...], w)


def _layer_kernel(tiles, sub, start, *refs):
    refs = list(refs)
    take = lambda k: [refs.pop(0) for _ in range(k)]
    x_ref, p_ref, xs_ref, ps_ref, sconv_ref, spool_ref = take(6)
    g_mix, w_conv, pool_scale, g_mlp, g_ple, g_final, w_pool_f32 = take(7)
    hbm = take(len(_STREAMED))
    y_ref, nconv_ref, npool_ref, ys_ref, nconv_s_ref, npool_s_ref = take(6)
    resident = take(len(_STREAMED))
    w_pool, stage, sems = take(3)
    m_prompt = _Mixer(*take(2))
    m_sample = _Mixer(*take(2))
    assert not refs
    w = _Weights(g_mix, w_conv, pool_scale, g_mlp, g_ple, g_final, w_pool, *resident)
    step = pl.program_id(0)

    @pl.when(step == 0)
    def _():
        stream = _WeightStream(hbm, resident, stage, sems)
        for g in range(len(POOL_WINDOWS)):
            w_pool[g] = _pack(w_pool_f32[g])
        m_prompt.ubuf[0:U_SLOT, :] = jnp.zeros((U_SLOT, POOL_WIDTH), F32)
        m_prompt.cvbuf[0:CV_SLOT, :] = jnp.zeros((CV_SLOT, CONV_WIDTH), F32)
        _prompt_tile(step, sub, x_ref, p_ref, y_ref, nconv_ref, npool_ref, m_prompt, w, stream)
        stream.finish()

    @pl.when(jnp.logical_and(step > 0, step < tiles))
    def _():
        _prompt_tile(step, sub, x_ref, p_ref, y_ref, nconv_ref, npool_ref, m_prompt, w)

    @pl.when(step == tiles)
    def _():
        _sample_rows(start, xs_ref, ps_ref, sconv_ref, spool_ref, ys_ref, nconv_s_ref, npool_s_ref,
                     m_sample, w)


def _resident(shape):
    zeros = (0,) * len(shape)
    return pl.BlockSpec(shape, lambda i: zeros, pipeline_mode=pl.Buffered(1))


def _whole(shape):
    zeros = (0,) * len(shape)
    return pl.BlockSpec(shape, lambda i: zeros)


def _layer_call(x, p, xs, ps, sconv, spool, small, streamed, tile, sub, start):
    seq, rows = x.shape[0], xs.shape[0]
    nb = sconv.shape[0]
    n = rows // nb
    assert seq % tile == 0 and tile % sub == 0 and sub >= U_SLOT
    tiles = seq // tile
    row_spec = lambda cols: pl.BlockSpec((tile, cols), lambda i: (jnp.minimum(i, tiles - 1), 0))
    f32_out = lambda *shape: jax.ShapeDtypeStruct(shape, F32)
    mixer_scratch = lambda lead, r: [pltpu.VMEM(lead + (U_SLOT + r, POOL_WIDTH), F32),
                                     pltpu.VMEM(lead + (CV_SLOT + r, CONV_WIDTH), F32)]
    return pl.pallas_call(
        functools.partial(_layer_kernel, tiles, sub, start),
        grid=(tiles + 1,),
        in_specs=([row_spec(D_MODEL), row_spec(PLE_DIM)]
                  + [_resident(v.shape) for v in (xs, ps, sconv, spool) + tuple(small)]
                  + [pl.BlockSpec(memory_space=pl.ANY)] * len(streamed)),
        out_specs=[row_spec(D_MODEL), _whole((CONV_HIST, CONV_WIDTH)),
                   _whole((POOL_HIST, 1, POOL_WIDTH)), _whole((rows, D_MODEL)),
                   _whole((nb, CONV_HIST, CONV_WIDTH)), _whole((POOL_HIST, nb, POOL_WIDTH))],
        out_shape=[f32_out(seq, D_MODEL), f32_out(CONV_HIST, CONV_WIDTH),
                   f32_out(POOL_HIST, 1, POOL_WIDTH), f32_out(rows, D_MODEL),
                   f32_out(nb, CONV_HIST, CONV_WIDTH), f32_out(POOL_HIST, nb, POOL_WIDTH)],
        scratch_shapes=([pltpu.VMEM((v.shape[0] // 2, v.shape[1]), jnp.uint32) for v in streamed]
                        + [pltpu.VMEM((len(POOL_WINDOWS), POOL_GROUP // 2, POOL_GROUP),
                                      jnp.uint32),
                           pltpu.VMEM((STAGE_SLOTS, STAGE_ROWS, STAGE_COLS), F32),
                           pltpu.SemaphoreType.DMA((STAGE_SLOTS,))]
                        + mixer_scratch((), tile) + mixer_scratch((nb,), n)),
        compiler_params=pltpu.CompilerParams(dimension_semantics=("arbitrary",),
                                             vmem_limit_bytes=VMEM_LIMIT_BYTES),
        name="encoder_layer",
    )(x, p, xs, ps, sconv, spool, *small, *streamed)


def kernel(x_prompt, x_sample, state_conv, state_pool, p_prompt, p_sample, g_mix, w_in, w_conv,
           w_pool, pool_scale, w_out, g_mlp, w_up, w_down, g_ple, w_ple, w_ple_gate, g_final):
    assert g_mix.shape[0] == 1, "single-layer trunk"
    batch, seq, _ = x_prompt.shape
    assert batch == 1
    dec_batch, dec_seq, _ = x_sample.shape

    drop = lambda v: v.reshape(v.shape[1:])
    time_major = lambda v: jnp.swapaxes(v, -3, -2)
    small = (g_mix, time_major(w_conv), pool_scale, g_mlp, g_ple, g_final.reshape(1, -1),
             drop(w_pool))
    streamed = tuple(drop(v) for v in (w_in, w_out, w_up, w_down, w_ple, w_ple_gate))

    yp, cp, pp, ys, cs, ps = _layer_call(
        x_prompt.reshape(seq, D_MODEL), p_prompt.reshape(seq, PLE_DIM),
        x_sample.reshape(dec_batch * dec_seq, D_MODEL),
        p_sample.reshape(dec_batch * dec_seq, PLE_DIM), drop(state_conv),
        drop(time_major(state_pool)), small, streamed, PROMPT_TILE, PROMPT_SUB_TILE, PAST_LEN)
    return (yp.reshape(1, seq, D_MODEL), ys.reshape(dec_batch, dec_seq, D_MODEL),
            cp.reshape(1, 1, CONV_HIST, CONV_WIDTH),
            time_major(pp.reshape(1, POOL_HIST, 1, POOL_WIDTH)),
            cs.reshape(1, dec_batch, CONV_HIST, CONV_WIDTH),
            time_major(ps.reshape(1, POOL_HIST, dec_batch, POOL_WIDTH)))
```
